```python
import math
import jax
import jax.numpy as jnp
from jax import lax
import numpy as np

D_MODEL = 4096
BATCH = 2
SEQ = 4096
DEPTH = 2

CTX_LEN = 256
GRID_W = 64
N_BRANCH = 4
EPS = 1e-6
F32 = jnp.float32

FNET_GROUPS = 4
FNET_GW = 256
FNET_W = FNET_GROUPS * FNET_GW
DIFF_HEADS = 8
DIFF_DK = 64
DIFF_DV = 128
DIFF_QW = DIFF_HEADS * 2 * DIFF_DK
DIFF_W = DIFF_HEADS * DIFF_DV
DIFF_SCALE = DIFF_DK ** -0.5
Q_BLOCK = 128
ROPE_BASE = 10000.0
ROPE_FREQS = DIFF_DK // 4
SSD_HEADS = 16
SSD_HD = 64
SSD_W = SSD_HEADS * SSD_HD
SSD_STATE = 128
SSD_GROUPS = 4
SSD_BC = SSD_GROUPS * SSD_STATE
SSD_CONV = 5
SSD_CHUNK = 128
CONF_W = 1024
CONF_K = 31
N_EXPERTS = 64
TOP_K = 6
N_EXPERT_GROUPS = 8
TOPK_GROUPS = 4
EXPERT_W = 256
SHARED_W = 1024
ROUTED_SCALE = 2.5
MOE_BLOCK = 128

K_OFF = 0
V_OFF = K_OFF + DIFF_QW
SX_OFF = V_OFF + DIFF_W
SB_OFF = SX_OFF + SSD_W
SDT_OFF = SB_OFF + SSD_BC
SC_OFF = SDT_OFF + 2 * SSD_HEADS
Q_OFF = SC_OFF + SSD_BC
SZ_OFF = Q_OFF + DIFF_QW
FN_OFF = SZ_OFF + SSD_W
CF_OFF = FN_OFF + FNET_W
GATE_OFF = CF_OFF + 2 * CONF_W
N_IN = GATE_OFF + N_BRANCH * D_MODEL
CTX_STATE_COLS = SC_OFF

kernel_name = 'hybrid_parallel_diffusion_block'


def rmsnorm(x, g):
    xf = x.astype(F32)
    y = xf * lax.rsqrt(jnp.mean(xf * xf, axis=-1, keepdims=True) + EPS)
    return (y * g.astype(F32)).astype(x.dtype)


def layernorm(x, g, b):
    xf = x.astype(F32)
    mu = jnp.mean(xf, axis=-1, keepdims=True)
    var = jnp.mean(jnp.square(xf - mu), axis=-1, keepdims=True)
    y = (xf - mu) * lax.rsqrt(var + EPS)
    return (y * g.astype(F32) + b.astype(F32)).astype(x.dtype)


def dwconv_centred(u, w, b):
    k = w.shape[0]
    pad = k // 2
    y = lax.conv_general_dilated(u, w[:, None, :].astype(u.dtype), window_strides=(1,),
                                 padding=[(pad, pad)], dimension_numbers=('NWC', 'WIO', 'NWC'),
                                 feature_group_count=u.shape[-1])
    return y + b.astype(u.dtype)


def flip(t):
    return jnp.flip(t, axis=1)


def axial_rope_tables(s):
    rows = s // GRID_W
    r, col = jnp.meshgrid(jnp.arange(rows, dtype=F32), jnp.arange(GRID_W, dtype=F32), indexing='ij')
    inv = ROPE_BASE ** (-jnp.arange(ROPE_FREQS, dtype=F32) / ROPE_FREQS)
    ang_r = r.reshape(-1)[:, None] * inv
    ang_c = col.reshape(-1)[:, None] * inv
    return jnp.cos(ang_r), jnp.sin(ang_r), jnp.cos(ang_c), jnp.sin(ang_c)


def _rotate_half(u, cs, sn):
    u1, u2 = jnp.split(u, 2, axis=-1)
    return jnp.concatenate([u1 * cs - u2 * sn, u2 * cs + u1 * sn], axis=-1)


def rope2d(t, tabs):
    cr, sr, cc, sc = [a[None, :, None, None, :] for a in tabs]
    tf = t.astype(F32)
    half = DIFF_DK // 2
    out = jnp.concatenate([_rotate_half(tf[..., :half], cr, sr), _rotate_half(tf[..., half:], cc, sc)], axis=-1)
    return out.astype(t.dtype)


def diff_attend(q, k, v, lam):
    s = jnp.einsum('bqhcd,bkhcd->bhcqk', q, k).astype(F32) * DIFF_SCALE
    pr = jax.nn.softmax(s, axis=-1)
    a = pr[:, :, 0] - lam * pr[:, :, 1]
    return jnp.einsum('bhqk,bkhd->bqhd', a.astype(v.dtype), v)


def diff_head_norm(o, g, lam_init):
    bsz, l = o.shape[:2]
    of = o.astype(F32)
    of = of * lax.rsqrt(jnp.mean(of * of, axis=-1, keepdims=True) + EPS) * g.astype(F32) * (1.0 - lam_init)
    return of.reshape(bsz, l, DIFF_W).astype(o.dtype)


def fourier_mix(u):
    bsz, l, _ = u.shape
    g = u.astype(F32).reshape(bsz, l, FNET_GROUPS, FNET_GW)
    y = jnp.fft.fft2(g, axes=(1, 3), norm='ortho').real
    return y.reshape(bsz, l, FNET_W).astype(u.dtype)


def conformer_conv(u, p):
    a, gt = jnp.split(u, 2, axis=-1)
    v = a * jax.nn.sigmoid(gt)
    v = dwconv_centred(v, p['conf_conv_w'], p['conf_conv_b'])
    v = layernorm(v, p['conf_ln_g'], p['conf_ln_b'])
    return jax.nn.silu(v)


def segsum(a):
    t = a.shape[-1]
    rep = jnp.broadcast_to(a[..., :, None], a.shape + (t,))
    rep = jnp.where(jnp.tril(jnp.ones((t, t), bool), -1), rep, 0.0)
    cs = jnp.cumsum(rep, axis=-2)
    return jnp.where(jnp.tril(jnp.ones((t, t), bool)), cs, -jnp.inf)


def ssd_scan(x, dt, a, bm, cm, init):
    bsz, l, h, pdim = x.shape
    n = bm.shape[-1]
    nc = l // SSD_CHUNK
    xd = (x.astype(F32) * dt[..., None]).reshape(bsz, nc, SSD_CHUNK, h, pdim)
    bc = bm.astype(F32).reshape(bsz, nc, SSD_CHUNK, h, n)
    ad = jnp.transpose((dt * a).reshape(bsz, nc, SSD_CHUNK, h), (0, 3, 1, 2))
    a_cs = jnp.cumsum(ad, axis=-1)
    decay_to_end = jnp.exp(a_cs[..., -1:] - a_cs)
    chunk_states = jnp.einsum('bclhn,bhcl,bclhp->bchpn', bc, decay_to_end, xd)
    states = jnp.concatenate([init[:, None].astype(F32), chunk_states], axis=1)
    chunk_decay = jnp.exp(segsum(jnp.pad(a_cs[..., -1], ((0, 0), (0, 0), (1, 0)))))
    states = jnp.einsum('bhzc,bchpn->bzhpn', chunk_decay, states)
    final = states[:, -1]
    if cm is None:
        return None, final
    cc = cm.astype(F32).reshape(bsz, nc, SSD_CHUNK, h, n)
    y_diag = jnp.einsum('bclhn,bcshn,bhcls,bcshp->bclhp', cc, bc, jnp.exp(segsum(ad)), xd)
    y_off = jnp.einsum('bclhn,bchpn,bhcl->bclhp', cc, states[:, :-1], jnp.exp(a_cs))
    return (y_diag + y_off).reshape(bsz, l, h, pdim), final


def ssd_prepare(xbc_raw, dt_raw, p):
    bsz, l, ch = xbc_raw.shape
    u = jax.nn.silu(dwconv_centred(xbc_raw, p['ssd_conv_w'][:, :ch], p['ssd_conv_b'][:ch]))
    xs = u[..., :SSD_W].reshape(bsz, l, SSD_HEADS, SSD_HD)

    def to_heads(t):
        return jnp.repeat(t.reshape(bsz, l, SSD_GROUPS, SSD_STATE), SSD_HEADS // SSD_GROUPS, axis=2)

    bs = to_heads(u[..., SSD_W:SSD_W + SSD_BC])
    cs = to_heads(u[..., SSD_W + SSD_BC:]) if ch > SSD_W + SSD_BC else None
    dt = jax.nn.softplus(dt_raw.astype(F32).reshape(bsz, l, 2, SSD_HEADS) + p['ssd_dt_bias'].astype(F32))
    return xs, bs, cs, dt


def ssd_finish(y, xs, z, p):
    bsz, l = z.shape[:2]
    y = (y + p['ssd_d'].astype(F32)[:, None] * xs.astype(F32)).reshape(bsz, l, SSD_W)
    yg = (y * jax.nn.silu(z.astype(F32))).reshape(bsz, l, SSD_GROUPS, SSD_W // SSD_GROUPS)
    yg = yg * lax.rsqrt(jnp.mean(yg * yg, axis=-1, keepdims=True) + EPS)
    return (yg.reshape(bsz, l, SSD_W) * p['ssd_norm_g'].astype(F32)).astype(z.dtype)


def merge_branches(branches, gate_logits, w_branch, w_out):
    acc = None
    for i, y in enumerate(branches):
        gi = jax.nn.sigmoid(gate_logits[..., i * D_MODEL:(i + 1) * D_MODEL])
        term = gi * (y @ w_branch[i])
        acc = term if acc is None else acc + term
    return acc @ w_out


def token_mixers(h, hc, p, layer_idx, ctx_out):
    bsz, s, _ = h.shape
    t = hc.shape[1]
    proj = h @ p['w_in']
    projc = hc @ (p['w_in'] if ctx_out else p['w_in'][:, :CTX_STATE_COLS])

    lam_init = 0.8 - 0.6 * math.exp(-0.3 * layer_idx)
    lam = (jnp.exp(jnp.sum(p['lam_q1'].astype(F32) * p['lam_k1'].astype(F32)))
           - jnp.exp(jnp.sum(p['lam_q2'].astype(F32) * p['lam_k2'].astype(F32))) + lam_init)
    tabs = axial_rope_tables(s)
    q = rope2d(proj[..., Q_OFF:SZ_OFF].reshape(bsz, s, DIFF_HEADS, 2, DIFF_DK), tabs)
    k = rope2d(proj[..., K_OFF:V_OFF].reshape(bsz, s, DIFF_HEADS, 2, DIFF_DK), tabs)
    v = proj[..., V_OFF:SX_OFF].reshape(bsz, s, DIFF_HEADS, DIFF_DV)
    kc = projc[..., K_OFF:V_OFF].reshape(bsz, t, DIFF_HEADS, 2, DIFF_DK)
    vc = projc[..., V_OFF:SX_OFF].reshape(bsz, t, DIFF_HEADS, DIFF_DV)
    k_all = jnp.concatenate([k, kc], axis=1)
    v_all = jnp.concatenate([v, vc], axis=1)
    qb = jnp.swapaxes(q.reshape(bsz, s // Q_BLOCK, Q_BLOCK, DIFF_HEADS, 2, DIFF_DK), 0, 1)
    o = lax.map(lambda qq: diff_attend(qq, k_all, v_all, lam), qb)
    o = jnp.swapaxes(o, 0, 1).reshape(bsz, s, DIFF_HEADS, DIFF_DV)
    y_diff = diff_head_norm(o, p['g_subln'], lam_init)

    a_ssd = -jnp.exp(p['ssd_a_log'].astype(F32))
    xs, bs, cs, dts = ssd_prepare(jnp.concatenate([proj[..., SX_OFF:SDT_OFF], proj[..., SC_OFF:Q_OFF]], axis=-1),
                                  proj[..., SDT_OFF:SC_OFF], p)
    xbc_c = (jnp.concatenate([projc[..., SX_OFF:SDT_OFF], projc[..., SC_OFF:Q_OFF]], axis=-1)
             if ctx_out else projc[..., SX_OFF:SDT_OFF])
    xsc, bsc, csc, dtsc = ssd_prepare(xbc_c, projc[..., SDT_OFF:SC_OFF], p)
    zero = jnp.zeros((bsz, SSD_HEADS, SSD_HD, SSD_STATE), F32)
    yc_f, st_f = ssd_scan(xsc, dtsc[:, :, 0], a_ssd[0], bsc, csc, zero)
    yc_b, st_b = ssd_scan(flip(xsc), flip(dtsc[:, :, 1]), a_ssd[1], flip(bsc),
                          None if csc is None else flip(csc), zero)
    y_f, _ = ssd_scan(xs, dts[:, :, 0], a_ssd[0], bs, cs, st_f)
    y_b, _ = ssd_scan(flip(xs), flip(dts[:, :, 1]), a_ssd[1], flip(bs), flip(cs), st_b)
    y_ssd = ssd_finish(y_f + flip(y_b), xs, proj[..., SZ_OFF:FN_OFF], p)

    y_fn = fourier_mix(proj[..., FN_OFF:CF_OFF])
    y_cf = conformer_conv(proj[..., CF_OFF:GATE_OFF], p)

    out = merge_branches([y_fn, y_diff, y_ssd, y_cf], proj[..., GATE_OFF:], p['w_branch'], p['w_out'])
    if not ctx_out:
        return out, None

    qc = projc[..., Q_OFF:SZ_OFF].reshape(bsz, t, DIFF_HEADS, 2, DIFF_DK)
    yc_diff = diff_head_norm(diff_attend(qc, kc, vc, lam), p['g_subln'], lam_init)
    yc_ssd = ssd_finish(yc_f + flip(yc_b), xsc, projc[..., SZ_OFF:FN_OFF], p)
    yc_fn = fourier_mix(projc[..., FN_OFF:CF_OFF])
    yc_cf = conformer_conv(projc[..., CF_OFF:GATE_OFF], p)
    outc = merge_branches([yc_fn, yc_diff, yc_ssd, yc_cf], projc[..., GATE_OFF:], p['w_branch'], p['w_out'])
    return out, outc


def routed_ffn(h, p):
    bsz, s, _ = h.shape
    scores = jax.nn.sigmoid((h @ p['w_router']).astype(F32))
    sel = scores + p['router_bias'].astype(F32)
    grp = sel.reshape(bsz, s, N_EXPERT_GROUPS, N_EXPERTS // N_EXPERT_GROUPS)
    grp_score = jnp.sum(lax.top_k(grp, 2)[0], axis=-1)
    _, gidx = lax.top_k(grp_score, TOPK_GROUPS)
    gmask = jnp.any(gidx[..., :, None] == jnp.arange(N_EXPERT_GROUPS), axis=-2)
    emask = jnp.repeat(gmask, N_EXPERTS // N_EXPERT_GROUPS, axis=-1)
    _, eidx = lax.top_k(jnp.where(emask, sel, -jnp.inf), TOP_K)
    w = jnp.take_along_axis(scores, eidx, axis=-1)
    w = w / jnp.sum(w, axis=-1, keepdims=True) * ROUTED_SCALE
    comb = jnp.sum(jax.nn.one_hot(eidx, N_EXPERTS, dtype=F32) * w[..., None], axis=-2)

    hb = jnp.swapaxes(h.reshape(bsz, s // MOE_BLOCK, MOE_BLOCK, D_MODEL), 0, 1)
    cb = jnp.swapaxes(comb.astype(h.dtype).reshape(bsz, s // MOE_BLOCK, MOE_BLOCK, N_EXPERTS), 0, 1)

    def expert_block(args):
        hx, cx = args
        gu = jnp.einsum('bqd,edf->bqef', hx, p['w_exp_gu'])
        gate, up = jnp.split(gu, 2, axis=-1)
        act = jax.nn.silu(gate) * up * cx[..., None]
        return jnp.einsum('bqef,efd->bqd', act, p['w_exp_down'])

    routed = jnp.swapaxes(lax.map(expert_block, (hb, cb)), 0, 1).reshape(bsz, s, D_MODEL)
    sg, su = jnp.split(h @ p['w_sh_gu'], 2, axis=-1)
    shared = (jax.nn.silu(sg) * su) @ p['w_sh_down']
    return routed + shared


def trunk_layer(x, xc, c, c_ctx, p, layer_idx, ctx_out):
    mod = jax.nn.silu(c) @ p['w_ada'] + p['b_ada']
    modc = jax.nn.silu(c_ctx) @ p['w_ada'] + p['b_ada']
    sh1, sc1, g1, sh2, sc2, g2 = jnp.split(mod[:, None, :], 6, axis=-1)
    sh1c, sc1c, g1c, sh2c, sc2c, g2c = jnp.split(modc, 6, axis=-1)
    h = rmsnorm(x, p['g_mix']) * (1.0 + sc1) + sh1
    hc = rmsnorm(xc, p['g_mix']) * (1.0 + sc1c) + sh1c
    mix, mixc = token_mixers(h, hc, p, layer_idx, ctx_out)
    x = x + g1 * mix
    x = x + g2 * routed_ffn(rmsnorm(x, p['g_ffn']) * (1.0 + sc2) + sh2, p)
    if ctx_out:
        xc = xc + g1c * mixc
        xc = xc + g2c * routed_ffn(rmsnorm(xc, p['g_ffn']) * (1.0 + sc2c) + sh2c, p)
    return x, xc


def setup_inputs(seed: int = 0) -> dict:
    key = jax.random.key(seed)
    ks = jax.random.split(key, 34)
    L = DEPTH

    def nrm(i, shape, scale):
        return scale * jax.random.normal(ks[i], shape, F32)

    dt0 = jnp.exp(jax.random.uniform(ks[15], (L, 2, SSD_HEADS), F32, math.log(1e-3), math.log(1e-1)))
    dt_bias = dt0 + jnp.log(-jnp.expm1(-dt0))
    a_log = jnp.log(jax.random.uniform(ks[16], (L, 2, SSD_HEADS), F32, 1.0, 16.0))
    return {
        'x': nrm(0, (BATCH, SEQ, D_MODEL), 1.0),
        'c': nrm(1, (BATCH, D_MODEL), 1.0),
        'ctx': nrm(2, (BATCH, CTX_LEN, D_MODEL), 1.0),
        'c_ctx': nrm(3, (D_MODEL,), 1.0),
        'w_ada': nrm(4, (L, D_MODEL, 6 * D_MODEL), 0.5 * D_MODEL ** -0.5),
        'b_ada': nrm(5, (L, 6 * D_MODEL), 0.01),
        'g_mix': 1.0 + nrm(6, (L, D_MODEL), 0.05),
        'w_in': nrm(7, (L, D_MODEL, N_IN), D_MODEL ** -0.5),
        'lam_q1': nrm(8, (L, DIFF_DK), 0.1),
        'lam_k1': nrm(9, (L, DIFF_DK), 0.1),
        'lam_q2': nrm(10, (L, DIFF_DK), 0.1),
        'lam_k2': nrm(11, (L, DIFF_DK), 0.1),
        'g_subln': 1.0 + nrm(12, (L, DIFF_DV), 0.05),
        'ssd_conv_w': nrm(13, (L, SSD_CONV, SSD_W + 2 * SSD_BC), SSD_CONV ** -0.5),
        'ssd_conv_b': nrm(14, (L, SSD_W + 2 * SSD_BC), 0.01),
        'ssd_dt_bias': dt_bias,
        'ssd_a_log': a_log,
        'ssd_d': 1.0 + nrm(17, (L, SSD_HEADS), 0.05),
        'ssd_norm_g': 1.0 + nrm(18, (L, SSD_W), 0.05),
        'conf_conv_w': nrm(19, (L, CONF_K, CONF_W), CONF_K ** -0.5),
        'conf_conv_b': nrm(20, (L, CONF_W), 0.01),
        'conf_ln_g': 1.0 + nrm(21, (L, CONF_W), 0.05),
        'conf_ln_b': nrm(22, (L, CONF_W), 0.01),
        'w_branch': nrm(23, (L, N_BRANCH, FNET_W, D_MODEL), FNET_W ** -0.5),
        'w_out': nrm(24, (L, D_MODEL, D_MODEL), D_MODEL ** -0.5),
        'g_ffn': 1.0 + nrm(25, (L, D_MODEL), 0.05),
        'w_router': nrm(26, (L, D_MODEL, N_EXPERTS), D_MODEL ** -0.5),
        'router_bias': nrm(27, (L, N_EXPERTS), 0.01),
        'w_exp_gu': nrm(28, (L, N_EXPERTS, D_MODEL, 2 * EXPERT_W), D_MODEL ** -0.5),
        'w_exp_down': nrm(29, (L, N_EXPERTS, EXPERT_W, D_MODEL), EXPERT_W ** -0.5),
        'w_sh_gu': nrm(30, (L, D_MODEL, 2 * SHARED_W), D_MODEL ** -0.5),
        'w_sh_down': nrm(31, (L, SHARED_W, D_MODEL), SHARED_W ** -0.5),
        'g_final': 1.0 + nrm(32, (D_MODEL,), 0.05),
    }


def reference(x, c, ctx, c_ctx, w_ada, b_ada, g_mix, w_in, lam_q1, lam_k1, lam_q2, lam_k2, g_subln,
              ssd_conv_w, ssd_conv_b, ssd_dt_bias, ssd_a_log, ssd_d, ssd_norm_g,
              conf_conv_w, conf_conv_b, conf_ln_g, conf_ln_b, w_branch, w_out, g_ffn,
              w_router, router_bias, w_exp_gu, w_exp_down, w_sh_gu, w_sh_down, g_final):
    xc = ctx
    for l in range(DEPTH):
        p = {
            'w_ada': w_ada[l], 'b_ada': b_ada[l], 'g_mix': g_mix[l], 'w_in': w_in[l],
            'lam_q1': lam_q1[l], 'lam_k1': lam_k1[l], 'lam_q2': lam_q2[l], 'lam_k2': lam_k2[l],
            'g_subln': g_subln[l], 'ssd_conv_w': ssd_conv_w[l], 'ssd_conv_b': ssd_conv_b[l],
            'ssd_dt_bias': ssd_dt_bias[l], 'ssd_a_log': ssd_a_log[l], 'ssd_d': ssd_d[l],
            'ssd_norm_g': ssd_norm_g[l], 'conf_conv_w': conf_conv_w[l], 'conf_conv_b': conf_conv_b[l],
            'conf_ln_g': conf_ln_g[l], 'conf_ln_b': conf_ln_b[l], 'w_branch': w_branch[l],
            'w_out': w_out[l], 'g_ffn': g_ffn[l], 'w_router': w_router[l], 'router_bias': router_bias[l],
            'w_exp_gu': w_exp_gu[l], 'w_exp_down': w_exp_down[l], 'w_sh_gu': w_sh_gu[l],
            'w_sh_down': w_sh_down[l],
        }
        x, xc = trunk_layer(x, xc, c, c_ctx, p, l, l < DEPTH - 1)
    return rmsnorm(x, g_final)
```

```python
import functools
import math

import jax
import jax.numpy as jnp
from jax import lax
from jax.experimental import pallas as pl
from jax.experimental.pallas import tpu as pltpu

F32 = jnp.float32
BF16 = jnp.bfloat16

D_MODEL = 4096
DEPTH = 2
GRID_W = 64
N_BRANCH = 4
EPS = 1e-6
FNET_GROUPS = 4
FNET_GW = 256
FNET_W = FNET_GROUPS * FNET_GW
DIFF_HEADS = 8
DIFF_DK = 64
DIFF_DV = 128
DIFF_QW = DIFF_HEADS * 2 * DIFF_DK
DIFF_W = DIFF_HEADS * DIFF_DV
DIFF_SCALE = DIFF_DK ** -0.5
Q_BLOCK = 128
ROPE_BASE = 10000.0
ROPE_FREQS = DIFF_DK // 4
SSD_HEADS = 16
SSD_HD = 64
SSD_W = SSD_HEADS * SSD_HD
SSD_STATE = 128
SSD_GROUPS = 4
SSD_BC = SSD_GROUPS * SSD_STATE
SSD_CONV = 5
SSD_CHUNK = 128
CONF_W = 1024
CONF_K = 31
N_EXPERTS = 64
TOP_K = 6
N_EXPERT_GROUPS = 8
TOPK_GROUPS = 4
EXPERT_W = 256
SHARED_W = 1024
ROUTED_SCALE = 2.5
MOE_BLOCK = 128

K_OFF = 0
V_OFF = K_OFF + DIFF_QW
SX_OFF = V_OFF + DIFF_W
SB_OFF = SX_OFF + SSD_W
SDT_OFF = SB_OFF + SSD_BC
SC_OFF = SDT_OFF + 2 * SSD_HEADS
Q_OFF = SC_OFF + SSD_BC
SZ_OFF = Q_OFF + DIFF_QW
FN_OFF = SZ_OFF + SSD_W
CF_OFF = FN_OFF + FNET_W
GATE_OFF = CF_OFF + 2 * CONF_W
N_IN = GATE_OFF + N_BRANCH * D_MODEL

VMEM_LIMIT_BYTES = 56 * 1024 * 1024
LANE = 128


def _mm_kernel(a_ref, b_ref, o_ref):
    o_ref[...] = jnp.dot(a_ref[...], b_ref[...], preferred_element_type=F32).astype(o_ref.dtype)


def _mm_acc_kernel(a_ref, b_ref, o_ref, acc_ref):
    k = pl.program_id(2)

    @pl.when(k == 0)
    def _():
        acc_ref[...] = jnp.zeros_like(acc_ref)

    acc_ref[...] += jnp.dot(a_ref[...], b_ref[...], preferred_element_type=F32)

    @pl.when(k == pl.num_programs(2) - 1)
    def _():
        o_ref[...] = acc_ref[...].astype(o_ref.dtype)


def _pick_tile(n, pref):
    t = min(n, pref)
    while n % t:
        t //= 2
    return t


def matmul(a, b, out_dtype=F32, tm=1024, tn=512, tk=4096):
    a = a.astype(BF16)
    b = b.astype(BF16)
    m, kdim = a.shape
    n = b.shape[1]
    tm = _pick_tile(m, tm)
    tn = _pick_tile(n, tn)
    tk = _pick_tile(kdim, tk)
    params = dict(vmem_limit_bytes=VMEM_LIMIT_BYTES)
    if tk == kdim:
        return pl.pallas_call(
            _mm_kernel,
            grid=(m // tm, n // tn),
            in_specs=[pl.BlockSpec((tm, kdim), lambda i, j: (i, 0)),
                      pl.BlockSpec((kdim, tn), lambda i, j: (0, j))],
            out_specs=pl.BlockSpec((tm, tn), lambda i, j: (i, j)),
            out_shape=jax.ShapeDtypeStruct((m, n), out_dtype),
            compiler_params=pltpu.CompilerParams(
                dimension_semantics=("parallel", "parallel"), **params),
        )(a, b)
    return pl.pallas_call(
        _mm_acc_kernel,
        grid=(m // tm, n // tn, kdim // tk),
        in_specs=[pl.BlockSpec((tm, tk), lambda i, j, k: (i, k)),
                  pl.BlockSpec((tk, tn), lambda i, j, k: (k, j))],
        out_specs=pl.BlockSpec((tm, tn), lambda i, j, k: (i, j)),
        out_shape=jax.ShapeDtypeStruct((m, n), out_dtype),
        scratch_shapes=[pltpu.VMEM((tm, tn), F32)],
        compiler_params=pltpu.CompilerParams(
            dimension_semantics=("parallel", "parallel", "arbitrary"), **params),
    )(a, b)


def rmsnorm(x, g):
    xf = x.astype(F32)
    y = xf * lax.rsqrt(jnp.mean(xf * xf, axis=-1, keepdims=True) + EPS)
    return y * g.astype(F32)


def layernorm(x, g, b):
    mu = jnp.mean(x, axis=-1, keepdims=True)
    var = jnp.mean(jnp.square(x - mu), axis=-1, keepdims=True)
    return (x - mu) * lax.rsqrt(var + EPS) * g + b


def dwconv_centred(u, w, b):
    pad = w.shape[0] // 2
    y = lax.conv_general_dilated(u, w[:, None, :].astype(u.dtype), window_strides=(1,),
                                 padding=[(pad, pad)], dimension_numbers=('NWC', 'WIO', 'NWC'),
                                 feature_group_count=u.shape[-1])
    return y + b.astype(u.dtype)


def flip(t):
    return jnp.flip(t, axis=1)


def axial_rope_tables(s):
    rows = s // GRID_W
    r, col = jnp.meshgrid(jnp.arange(rows, dtype=F32), jnp.arange(GRID_W, dtype=F32), indexing='ij')
    inv = ROPE_BASE ** (-jnp.arange(ROPE_FREQS, dtype=F32) / ROPE_FREQS)
    ang_r = r.reshape(-1)[:, None] * inv
    ang_c = col.reshape(-1)[:, None] * inv
    return jnp.cos(ang_r), jnp.sin(ang_r), jnp.cos(ang_c), jnp.sin(ang_c)


def _rotate_half(u, cs, sn):
    u1, u2 = jnp.split(u, 2, axis=-1)
    return jnp.concatenate([u1 * cs - u2 * sn, u2 * cs + u1 * sn], axis=-1)


def rope2d(t, tabs):
    cr, sr, cc, sc = [a[None, :, None, None, :] for a in tabs]
    half = DIFF_DK // 2
    return jnp.concatenate([_rotate_half(t[..., :half], cr, sr), _rotate_half(t[..., half:], cc, sc)], axis=-1)


def diff_attend(q, k, v, lam):
    s = jnp.einsum('bqhcd,bkhcd->bhcqk', q, k).astype(F32) * DIFF_SCALE
    pr = jax.nn.softmax(s, axis=-1)
    a = pr[:, :, 0] - lam * pr[:, :, 1]
    return jnp.einsum('bhqk,bkhd->bqhd', a, v)


def diff_head_norm(o, g, lam_init):
    bsz, l = o.shape[:2]
    of = o * lax.rsqrt(jnp.mean(o * o, axis=-1, keepdims=True) + EPS) * g * (1.0 - lam_init)
    return of.reshape(bsz, l, DIFF_W)


def fourier_mix(u):
    bsz, l, _ = u.shape
    g = u.reshape(bsz, l, FNET_GROUPS, FNET_GW)
    y = jnp.fft.fft2(g, axes=(1, 3), norm='ortho').real
    return y.reshape(bsz, l, FNET_W)


def conformer_conv(u, p):
    a, gt = jnp.split(u, 2, axis=-1)
    v = a * jax.nn.sigmoid(gt)
    v = dwconv_centred(v, p['conf_conv_w'], p['conf_conv_b'])
    v = layernorm(v, p['conf_ln_g'], p['conf_ln_b'])
    return jax.nn.silu(v)


def segsum(a):
    t = a.shape[-1]
    rep = jnp.broadcast_to(a[..., :, None], a.shape + (t,))
    rep = jnp.where(jnp.tril(jnp.ones((t, t), bool), -1), rep, 0.0)
    cs = jnp.cumsum(rep, axis=-2)
    return jnp.where(jnp.tril(jnp.ones((t, t), bool)), cs, -jnp.inf)


def ssd_scan(x, dt, a, bm, cm, init):
    bsz, l, h, pdim = x.shape
    n = bm.shape[-1]
    nc = l // SSD_CHUNK
    xd = (x * dt[..., None]).reshape(bsz, nc, SSD_CHUNK, h, pdim)
    bc = bm.reshape(bsz, nc, SSD_CHUNK, h, n)
    ad = jnp.transpose((dt * a).reshape(bsz, nc, SSD_CHUNK, h), (0, 3, 1, 2))
    a_cs = jnp.cumsum(ad, axis=-1)
    decay_to_end = jnp.exp(a_cs[..., -1:] - a_cs)
    chunk_states = jnp.einsum('bclhn,bhcl,bclhp->bchpn', bc, decay_to_end, xd)
    states = jnp.concatenate([init[:, None], chunk_states], axis=1)
    chunk_decay = jnp.exp(segsum(jnp.pad(a_cs[..., -1], ((0, 0), (0, 0), (1, 0)))))
    states = jnp.einsum('bhzc,bchpn->bzhpn', chunk_decay, states)
    final = states[:, -1]
    if cm is None:
        return None, final
    cc = cm.reshape(bsz, nc, SSD_CHUNK, h, n)
    y_diag = jnp.einsum('bclhn,bcshn,bhcls,bcshp->bclhp', cc, bc, jnp.exp(segsum(ad)), xd)
    y_off = jnp.einsum('bclhn,bchpn,bhcl->bclhp', cc, states[:, :-1], jnp.exp(a_cs))
    return (y_diag + y_off).reshape(bsz, l, h, pdim), final


def ssd_prepare(xbc_raw, dt_raw, p):
    bsz, l, ch = xbc_raw.shape
    u = jax.nn.silu(dwconv_centred(xbc_raw, p['ssd_conv_w'][:, :ch], p['ssd_conv_b'][:ch]))
    xs = u[..., :SSD_W].reshape(bsz, l, SSD_HEADS, SSD_HD)

    def to_heads(t):
        return jnp.repeat(t.reshape(bsz, l, SSD_GROUPS, SSD_STATE), SSD_HEADS // SSD_GROUPS, axis=2)

    bs = to_heads(u[..., SSD_W:SSD_W + SSD_BC])
    cs = to_heads(u[..., SSD_W + SSD_BC:]) if ch > SSD_W + SSD_BC else None
    dt = jax.nn.softplus(dt_raw.reshape(bsz, l, 2, SSD_HEADS) + p['ssd_dt_bias'])
    return xs, bs, cs, dt


def ssd_finish(y, xs, z, p):
    bsz, l = z.shape[:2]
    y = (y + p['ssd_d'][:, None] * xs).reshape(bsz, l, SSD_W)
    yg = (y * jax.nn.silu(z)).reshape(bsz, l, SSD_GROUPS, SSD_W // SSD_GROUPS)
    yg = yg * lax.rsqrt(jnp.mean(yg * yg, axis=-1, keepdims=True) + EPS)
    return yg.reshape(bsz, l, SSD_W) * p['ssd_norm_g']


def merge_branches(branches, gate_logits, w_branch, w_out):
    bsz, l, _ = gate_logits.shape
    acc = None
    for i, y in enumerate(branches):
        gi = jax.nn.sigmoid(gate_logits[..., i * D_MODEL:(i + 1) * D_MODEL])
        term = gi * matmul(y.reshape(bsz * l, -1), w_branch[i]).reshape(bsz, l, D_MODEL)
        acc = term if acc is None else acc + term
    return matmul(acc.reshape(bsz * l, D_MODEL), w_out).reshape(bsz, l, D_MODEL)


def project(h, w):
    bsz, l, d = h.shape
    return matmul(h.reshape(bsz * l, d), w).reshape(bsz, l, -1)


def token_mixers(h, hc, p, layer_idx, ctx_out):
    bsz, s, _ = h.shape
    t = hc.shape[1]
    w_in = p['w_in']
    proj_a = project(h, w_in[:, :SDT_OFF])
    proj_dt = project(h, jnp.pad(w_in[:, SDT_OFF:SC_OFF], ((0, 0), (0, LANE - 2 * SSD_HEADS))))[..., :2 * SSD_HEADS]
    proj_b = project(h, w_in[:, SC_OFF:GATE_OFF])
    proj_g = project(h, w_in[:, GATE_OFF:])
    proj = jnp.concatenate([proj_a, proj_dt, proj_b], axis=-1)
    pc_a = project(hc, w_in[:, :SDT_OFF])
    pc_dt = project(hc, jnp.pad(w_in[:, SDT_OFF:SC_OFF], ((0, 0), (0, LANE - 2 * SSD_HEADS))))[..., :2 * SSD_HEADS]
    if ctx_out:
        pc_b = project(hc, w_in[:, SC_OFF:GATE_OFF])
        pc_g = project(hc, w_in[:, GATE_OFF:])
        projc = jnp.concatenate([pc_a, pc_dt, pc_b], axis=-1)
    else:
        projc = jnp.concatenate([pc_a, pc_dt], axis=-1)

    lam_init = 0.8 - 0.6 * math.exp(-0.3 * layer_idx)
    lam = (jnp.exp(jnp.sum(p['lam_q1'] * p['lam_k1'])) - jnp.exp(jnp.sum(p['lam_q2'] * p['lam_k2'])) + lam_init)
    tabs = axial_rope_tables(s)
    q = rope2d(proj[..., Q_OFF:SZ_OFF].reshape(bsz, s, DIFF_HEADS, 2, DIFF_DK), tabs)
    k = rope2d(proj[..., K_OFF:V_OFF].reshape(bsz, s, DIFF_HEADS, 2, DIFF_DK), tabs)
    v = proj[..., V_OFF:SX_OFF].reshape(bsz, s, DIFF_HEADS, DIFF_DV)
    kc = projc[..., K_OFF:V_OFF].reshape(bsz, t, DIFF_HEADS, 2, DIFF_DK)
    vc = projc[..., V_OFF:SX_OFF].reshape(bsz, t, DIFF_HEADS, DIFF_DV)
    k_all = jnp.concatenate([k, kc], axis=1)
    v_all = jnp.concatenate([v, vc], axis=1)
    qb = jnp.swapaxes(q.reshape(bsz, s // Q_BLOCK, Q_BLOCK, DIFF_HEADS, 2, DIFF_DK), 0, 1)
    o = lax.map(lambda qq: diff_attend(qq, k_all, v_all, lam), qb)
    o = jnp.swapaxes(o, 0, 1).reshape(bsz, s, DIFF_HEADS, DIFF_DV)
    y_diff = diff_head_norm(o, p['g_subln'], lam_init)

    a_ssd = -jnp.exp(p['ssd_a_log'])
    xs, bs, cs, dts = ssd_prepare(jnp.concatenate([proj[..., SX_OFF:SDT_OFF], proj[..., SC_OFF:Q_OFF]], axis=-1),
                                  proj[..., SDT_OFF:SC_OFF], p)
    xbc_c = (jnp.concatenate([projc[..., SX_OFF:SDT_OFF], projc[..., SC_OFF:Q_OFF]], axis=-1)
             if ctx_out else projc[..., SX_OFF:SDT_OFF])
    xsc, bsc, csc, dtsc = ssd_prepare(xbc_c, projc[..., SDT_OFF:SC_OFF], p)
    zero = jnp.zeros((bsz, SSD_HEADS, SSD_HD, SSD_STATE), F32)
    yc_f, st_f = ssd_scan(xsc, dtsc[:, :, 0], a_ssd[0], bsc, csc, zero)
    yc_b, st_b = ssd_scan(flip(xsc), flip(dtsc[:, :, 1]), a_ssd[1], flip(bsc),
                          None if csc is None else flip(csc), zero)
    y_f, _ = ssd_scan(xs, dts[:, :, 0], a_ssd[0], bs, cs, st_f)
    y_b, _ = ssd_scan(flip(xs), flip(dts[:, :, 1]), a_ssd[1], flip(bs), flip(cs), st_b)
    y_ssd = ssd_finish(y_f + flip(y_b), xs, proj[..., SZ_OFF:FN_OFF], p)

    y_fn = fourier_mix(proj[..., FN_OFF:CF_OFF])
    y_cf = conformer_conv(proj[..., CF_OFF:GATE_OFF], p)

    out = merge_branches([y_fn, y_diff, y_ssd, y_cf], proj_g, p['w_branch'], p['w_out'])
    if not ctx_out:
        return out, None

    qc = projc[..., Q_OFF:SZ_OFF].reshape(bsz, t, DIFF_HEADS, 2, DIFF_DK)
    yc_diff = diff_head_norm(diff_attend(qc, kc, vc, lam), p['g_subln'], lam_init)
    yc_ssd = ssd_finish(yc_f + flip(yc_b), xsc, projc[..., SZ_OFF:FN_OFF], p)
    yc_fn = fourier_mix(projc[..., FN_OFF:CF_OFF])
    yc_cf = conformer_conv(projc[..., CF_OFF:GATE_OFF], p)
    outc = merge_branches([yc_fn, yc_diff, yc_ssd, yc_cf], pc_g, p['w_branch'], p['w_out'])
    return out, outc


def routed_ffn(h, p):
    bsz, s, _ = h.shape
    hf = h.reshape(bsz * s, D_MODEL)
    logits = jnp.dot(hf, p['w_router'], precision=lax.Precision.HIGHEST)
    scores = jax.nn.sigmoid(logits)
    sel = scores + p['router_bias']
    grp = sel.reshape(-1, N_EXPERT_GROUPS, N_EXPERTS // N_EXPERT_GROUPS)
    grp_score = jnp.sum(lax.top_k(grp, 2)[0], axis=-1)
    _, gidx = lax.top_k(grp_score, TOPK_GROUPS)
    gmask = jnp.any(gidx[..., :, None] == jnp.arange(N_EXPERT_GROUPS), axis=-2)
    emask = jnp.repeat(gmask, N_EXPERTS // N_EXPERT_GROUPS, axis=-1)
    _, eidx = lax.top_k(jnp.where(emask, sel, -jnp.inf), TOP_K)
    w = jnp.take_along_axis(scores, eidx, axis=-1)
    w = w / jnp.sum(w, axis=-1, keepdims=True) * ROUTED_SCALE
    comb = jnp.sum(jax.nn.one_hot(eidx, N_EXPERTS, dtype=F32) * w[..., None], axis=-2)

    w_gu = jnp.transpose(p['w_exp_gu'].astype(BF16), (1, 0, 2)).reshape(D_MODEL, N_EXPERTS * 2 * EXPERT_W)
    gu = matmul(hf, w_gu, out_dtype=BF16).reshape(-1, N_EXPERTS, 2 * EXPERT_W).astype(F32)
    act = jax.nn.silu(gu[..., :EXPERT_W]) * gu[..., EXPERT_W:] * comb[..., None]
    routed = matmul(act.reshape(-1, N_EXPERTS * EXPERT_W),
                    p['w_exp_down'].reshape(N_EXPERTS * EXPERT_W, D_MODEL))
    sgu = matmul(hf, p['w_sh_gu'])
    shared = matmul(jax.nn.silu(sgu[:, :SHARED_W]) * sgu[:, SHARED_W:], p['w_sh_down'])
    return (routed + shared).reshape(bsz, s, D_MODEL)


def trunk_layer(x, xc, c, c_ctx, p, layer_idx, ctx_out):
    cc = jnp.concatenate([jax.nn.silu(c), jax.nn.silu(c_ctx)[None]], axis=0)
    cc = jnp.pad(cc, ((0, 16 - cc.shape[0]), (0, 0)))
    mods = matmul(cc, p['w_ada'], tn=1024)[:3] + p['b_ada']
    mod, modc = mods[:2], mods[2]
    sh1, sc1, g1, sh2, sc2, g2 = jnp.split(mod[:, None, :], 6, axis=-1)
    sh1c, sc1c, g1c, sh2c, sc2c, g2c = jnp.split(modc, 6, axis=-1)
    h = rmsnorm(x, p['g_mix']) * (1.0 + sc1) + sh1
    hc = rmsnorm(xc, p['g_mix']) * (1.0 + sc1c) + sh1c
    mix, mixc = token_mixers(h, hc, p, layer_idx, ctx_out)
    x = x + g1 * mix
    x = x + g2 * routed_ffn(rmsnorm(x, p['g_ffn']) * (1.0 + sc2) + sh2, p)
    if ctx_out:
        xc = xc + g1c * mixc
        xc = xc + g2c * routed_ffn(rmsnorm(xc, p['g_ffn']) * (1.0 + sc2c) + sh2c, p)
    return x, xc


def kernel(x, c, ctx, c_ctx, w_ada, b_ada, g_mix, w_in, lam_q1, lam_k1, lam_q2, lam_k2, g_subln, ssd_conv_w, ssd_conv_b, ssd_dt_bias, ssd_a_log, ssd_d, ssd_norm_g, conf_conv_w, conf_conv_b, conf_ln_g, conf_ln_b, w_branch, w_out, g_ffn, w_router, router_bias, w_exp_gu, w_exp_down, w_sh_gu, w_sh_down, g_final):
    xc = ctx
    for l in range(DEPTH):
        p = {
            'w_ada': w_ada[l], 'b_ada': b_ada[l], 'g_mix': g_mix[l], 'w_in': w_in[l],
            'lam_q1': lam_q1[l], 'lam_k1': lam_k1[l], 'lam_q2': lam_q2[l], 'lam_k2': lam_k2[l],
            'g_subln': g_subln[l], 'ssd_conv_w': ssd_conv_w[l], 'ssd_conv_b': ssd_conv_b[l],
            'ssd_dt_bias': ssd_dt_bias[l], 'ssd_a_log': ssd_a_log[l], 'ssd_d': ssd_d[l],
            'ssd_norm_g': ssd_norm_g[l], 'conf_conv_w': conf_conv_w[l], 'conf_conv_b': conf_conv_b[l],
            'conf_ln_g': conf_ln_g[l], 'conf_ln_b': conf_ln_b[l], 'w_branch': w_branch[l],
            'w_out': w_out[l], 'g_ffn': g_ffn[l], 'w_router': w_router[l], 'router_bias': router_bias[l],
            'w_exp_gu': w_exp_gu[l], 'w_exp_down': w_exp_down[l], 'w_sh_gu': w_sh_gu[l],
            'w_sh_down': w_sh_down[l],
        }
        x, xc = trunk_layer(x, xc, c, c_ctx, p, l, l < DEPTH - 1)
    return rmsnorm(x, g_final)
```

```python
import functools
import math

import jax
import jax.numpy as jnp
from jax import lax
from jax.experimental import pallas as pl
from jax.experimental.pallas import tpu as pltpu

F32 = jnp.float32
BF16 = jnp.bfloat16

D_MODEL = 4096
BATCH = 2
SEQ = 4096
DEPTH = 2
CTX_LEN = 256
GRID_W = 64
N_BRANCH = 4
EPS = 1e-6
FNET_GROUPS = 4
FNET_GW = 256
FNET_W = FNET_GROUPS * FNET_GW
DIFF_HEADS = 8
DIFF_DK = 64
DIFF_DV = 128
DIFF_QW = DIFF_HEADS * 2 * DIFF_DK
DIFF_W = DIFF_HEADS * DIFF_DV
DIFF_SCALE = DIFF_DK ** -0.5
ROPE_BASE = 10000.0
ROPE_FREQS = DIFF_DK // 4
SSD_HEADS = 16
SSD_HD = 64
SSD_W = SSD_HEADS * SSD_HD
SSD_STATE = 128
SSD_GROUPS = 4
SSD_BC = SSD_GROUPS * SSD_STATE
SSD_CONV = 5
SSD_CHUNK = 128
CONF_W = 1024
CONF_K = 31
N_EXPERTS = 64
TOP_K = 6
N_EXPERT_GROUPS = 8
TOPK_GROUPS = 4
EXPERT_W = 256
SHARED_W = 1024
ROUTED_SCALE = 2.5

K_OFF = 0
V_OFF = K_OFF + DIFF_QW
SX_OFF = V_OFF + DIFF_W
SB_OFF = SX_OFF + SSD_W
SDT_OFF = SB_OFF + SSD_BC
SC_OFF = SDT_OFF + 2 * SSD_HEADS
Q_OFF = SC_OFF + SSD_BC
SZ_OFF = Q_OFF + DIFF_QW
FN_OFF = SZ_OFF + SSD_W
CF_OFF = FN_OFF + FNET_W
GATE_OFF = CF_OFF + 2 * CONF_W
N_IN = GATE_OFF + N_BRANCH * D_MODEL

B_SC = 0
B_Q = Q_OFF - SC_OFF
B_SZ = SZ_OFF - SC_OFF
B_FN = FN_OFF - SC_OFF
B_CF = CF_OFF - SC_OFF

N_LAT = BATCH * SEQ
N_CTX = BATCH * CTX_LEN
N_ROWS = N_LAT + N_CTX
N_KEYS = SEQ + CTX_LEN

VMEM_LIMIT_BYTES = 56 * 1024 * 1024
LANE = 128
ROW_TILE = 512
COL_TILE = 512
ATTN_Q_TILE = 256
MOE_TILE = 256
MOE_ROWS_MAX = None


def _segment_of_row_tile(i, tm):
    return jnp.minimum(i * tm // SEQ, BATCH)


def _ws_kernel(*refs, n_a, a_of_w, n_extra, epilogue):
    n_w = len(a_of_w)
    a_refs = refs[:n_a]
    w_refs = refs[n_a:n_a + n_w]
    extra_refs = refs[n_a + n_w:n_a + n_w + n_extra]
    o_ref = refs[n_a + n_w + n_extra]
    wb_refs = refs[n_a + n_w + n_extra + 1:]

    @pl.when(pl.program_id(1) == 0)
    def _():
        for w_ref, wb_ref in zip(w_refs, wb_refs):
            wb_ref[...] = w_ref[...].astype(BF16)

    a_vals = [a_ref[...] for a_ref in a_refs]
    dots = [jnp.dot(a_vals[ai], wb_ref[...], preferred_element_type=F32)
            for ai, wb_ref in zip(a_of_w, wb_refs)]
    o_ref[...] = epilogue(dots, [e[...] for e in extra_refs]).astype(o_ref.dtype)


def ws_matmul(a_list, w_list, a_of_w, epilogue, extras=(), *, m, n, out_dtype, name, tm=ROW_TILE, tn=COL_TILE):
    assert m % tm == 0 and n % tn == 0
    in_specs = []
    operands = []
    for a in a_list:
        in_specs.append(pl.BlockSpec((tm, a.shape[1]), lambda j, i: (i, 0)))
        operands.append(a)
    scratch = []
    for w, prefix, colfn in w_list:
        kdim = w.shape[-2]
        in_specs.append(pl.BlockSpec((None,) * len(prefix) + (kdim, tn),
                                     lambda j, i, prefix=prefix, colfn=colfn: prefix + (0, colfn(j))))
        operands.append(w)
        scratch.append(pltpu.VMEM((kdim, tn), BF16))
    for ex in extras:
        if ex[0] == 'tile':
            _, arr, colfn = ex
            in_specs.append(pl.BlockSpec((tm, tn), lambda j, i, colfn=colfn: (i, colfn(j))))
        else:
            _, arr = ex
            in_specs.append(pl.BlockSpec((None, 1, tn), lambda j, i: (_segment_of_row_tile(i, tm), 0, j)))
        operands.append(arr)
    kern = functools.partial(_ws_kernel, n_a=len(a_list), a_of_w=tuple(a_of_w), n_extra=len(extras),
                             epilogue=epilogue)
    return pl.pallas_call(
        kern,
        name=name,
        grid=(n // tn, m // tm),
        in_specs=in_specs,
        out_specs=pl.BlockSpec((tm, tn), lambda j, i: (i, j)),
        out_shape=jax.ShapeDtypeStruct((m, n), out_dtype),
        scratch_shapes=scratch,
        compiler_params=pltpu.CompilerParams(
            dimension_semantics=("parallel", "arbitrary"), vmem_limit_bytes=VMEM_LIMIT_BYTES),
    )(*operands)


def _ident(dots, extras):
    return dots[0]


def project(h, w, n, *, m, name, prefix=(), out_dtype=BF16, tn=COL_TILE):
    return ws_matmul([h], [(w, prefix, lambda j: j)], [0], _ident, m=m, n=n, out_dtype=out_dtype, tn=tn,
                     name=name)


def _ada_kernel(c_ref, w_ref, b_ref, o_ref):
    o_ref[...] = jnp.dot(c_ref[...], w_ref[...].astype(BF16), preferred_element_type=F32) + b_ref[...]


def ada_modulation(cc, w_ada, b_ada, tn=COL_TILE):
    nl, d, n = w_ada.shape
    return pl.pallas_call(
        _ada_kernel,
        name="ada_modulation",
        grid=(nl, n // tn),
        in_specs=[pl.BlockSpec((16, d), lambda l, j: (0, 0)),
                  pl.BlockSpec((None, d, tn), lambda l, j: (l, 0, j)),
                  pl.BlockSpec((None, 1, tn), lambda l, j: (l, 0, j))],
        out_specs=pl.BlockSpec((None, 16, tn), lambda l, j: (l, 0, j)),
        out_shape=jax.ShapeDtypeStruct((nl, 16, n), F32),
        compiler_params=pltpu.CompilerParams(
            dimension_semantics=("parallel", "parallel"), vmem_limit_bytes=VMEM_LIMIT_BYTES),
    )(cc, w_ada, b_ada.reshape(nl, 1, n))


def _rope(u, cos, sin_signed, first_half):
    partner = jnp.where(first_half, pltpu.roll(u, LANE - ROPE_FREQS, axis=1), pltpu.roll(u, ROPE_FREQS, axis=1))
    return u * cos + partner * sin_signed


def _attn_kernel(lam_ref, q_ref, k_ref, v_ref, cq_ref, sq_ref, ck_ref, sk_ref, g_ref, o_ref, k_s,
                 *, out_scale):
    lane = lax.broadcasted_iota(jnp.int32, (1, LANE), 1)
    first_half = (lane % (2 * ROPE_FREQS)) < ROPE_FREQS
    comp0 = lane < DIFF_DK

    @pl.when(pl.program_id(2) == 0)
    def _():
        k_s[...] = _rope(k_ref[...].astype(F32), ck_ref[...], sk_ref[...], first_half).astype(BF16)

    q = _rope(q_ref[...].astype(F32), cq_ref[...], sq_ref[...], first_half) * DIFF_SCALE
    k = k_s[...]
    lam = lam_ref[0]
    probs = []
    for c in range(2):
        qc = jnp.where(comp0 if c == 0 else jnp.logical_not(comp0), q, 0.0).astype(BF16)
        s = lax.dot_general(qc, k, (((1,), (1,)), ((), ())), preferred_element_type=F32)
        s = s - jnp.max(s, axis=-1, keepdims=True)
        p = jnp.exp(s)
        r = 1.0 / jnp.sum(p, axis=-1, keepdims=True)
        probs.append(p * (r if c == 0 else r * lam))
    a = (probs[0] - probs[1]).astype(BF16)
    o = jnp.dot(a, v_ref[...], preferred_element_type=F32)
    o = o * lax.rsqrt(jnp.mean(o * o, axis=-1, keepdims=True) + EPS) * (g_ref[...] * out_scale)
    o_ref[...] = o.astype(o_ref.dtype)


def diff_attention(q, kv_all, rope_q, rope_k, lam, g_subln, lam_init, *, q_col_block):
    bsz, lq = q.shape[:2]
    keys = kv_all.shape[1]
    tq = min(ATTN_Q_TILE, lq)
    cq, sq = rope_q
    ck, sk = rope_k
    const = dict(pipeline_mode=pl.Buffered(1))
    kern = functools.partial(_attn_kernel, out_scale=1.0 - lam_init)
    return pl.pallas_call(
        kern,
        name="diff_attention",
        grid=(bsz, DIFF_HEADS, lq // tq),
        in_specs=[pl.BlockSpec(memory_space=pltpu.SMEM),
                  pl.BlockSpec((None, tq, LANE), lambda b, h, i: (b, i, q_col_block + h)),
                  pl.BlockSpec((None, keys, LANE), lambda b, h, i: (b, 0, h)),
                  pl.BlockSpec((None, keys, LANE), lambda b, h, i: (b, 0, DIFF_HEADS + h)),
                  pl.BlockSpec((tq, LANE), lambda b, h, i: (i, 0)),
                  pl.BlockSpec((tq, LANE), lambda b, h, i: (i, 0)),
                  pl.BlockSpec((keys, LANE), lambda b, h, i: (0, 0), **const),
                  pl.BlockSpec((keys, LANE), lambda b, h, i: (0, 0), **const),
                  pl.BlockSpec((1, LANE), lambda b, h, i: (0, 0))],
        out_specs=pl.BlockSpec((None, tq, LANE), lambda b, h, i: (b, i, h)),
        out_shape=jax.ShapeDtypeStruct((bsz, lq, DIFF_W), BF16),
        scratch_shapes=[pltpu.VMEM((keys, LANE), BF16)],
        compiler_params=pltpu.CompilerParams(
            dimension_semantics=("parallel", "parallel", "arbitrary"), vmem_limit_bytes=VMEM_LIMIT_BYTES),
    )(lam.reshape(1).astype(F32), q, kv_all, kv_all, cq, sq, ck, sk, g_subln.reshape(1, LANE).astype(F32))


def rope_tables(s):
    pos = jnp.arange(s, dtype=jnp.int32)
    row = (pos // GRID_W).astype(F32)[:, None]
    col = (pos % GRID_W).astype(F32)[:, None]
    lane = jnp.arange(LANE, dtype=jnp.int32)
    d = lane % DIFF_DK
    inv = ROPE_BASE ** (-(d % ROPE_FREQS).astype(F32) / ROPE_FREQS)
    ang = jnp.where((d < DIFF_DK // 2)[None, :], row * inv[None, :], col * inv[None, :])
    sign = jnp.where((d % (2 * ROPE_FREQS)) < ROPE_FREQS, -1.0, 1.0)
    return jnp.cos(ang), jnp.sin(ang) * sign[None, :]


def _moe_kernel(te_ref, nu_ref, x_ref, wgu_ref, wdn_ref, y_ref, wgu_s, wdn_s):
    t = pl.program_id(0)
    prev = te_ref[jnp.maximum(t - 1, 0)]
    new_expert = jnp.logical_or(t == 0, te_ref[t] != prev)

    @pl.when(new_expert)
    def _():
        wgu_s[...] = wgu_ref[...].astype(BF16)
        wdn_s[...] = wdn_ref[...].astype(BF16)

    @pl.when(t < nu_ref[0])
    def _():
        gu = jnp.dot(x_ref[...], wgu_s[...], preferred_element_type=F32)
        gate = gu[:, :EXPERT_W]
        act = (gate * jax.nn.sigmoid(gate) * gu[:, EXPERT_W:]).astype(BF16)
        y_ref[...] = jnp.dot(act, wdn_s[...], preferred_element_type=F32).astype(y_ref.dtype)


def moe_grouped(x_sorted, tile_expert, n_used, w_gu, w_down, layer):
    r = x_sorted.shape[0]
    n_tiles = r // MOE_TILE

    def row_map(t, te, nu):
        return (jnp.minimum(t, nu[0] - 1), 0)

    return pl.pallas_call(
        _moe_kernel,
        name="moe_experts",
        grid_spec=pltpu.PrefetchScalarGridSpec(
            num_scalar_prefetch=2,
            grid=(n_tiles,),
            in_specs=[pl.BlockSpec((MOE_TILE, D_MODEL), row_map),
                      pl.BlockSpec((None, None, D_MODEL, 2 * EXPERT_W), lambda t, te, nu: (layer, te[t], 0, 0)),
                      pl.BlockSpec((None, None, EXPERT_W, D_MODEL), lambda t, te, nu: (layer, te[t], 0, 0))],
            out_specs=pl.BlockSpec((MOE_TILE, D_MODEL), row_map),
            scratch_shapes=[pltpu.VMEM((D_MODEL, 2 * EXPERT_W), BF16),
                            pltpu.VMEM((EXPERT_W, D_MODEL), BF16)]),
        out_shape=jax.ShapeDtypeStruct((r, D_MODEL), BF16),
        compiler_params=pltpu.CompilerParams(
            dimension_semantics=("arbitrary",), vmem_limit_bytes=VMEM_LIMIT_BYTES),
    )(tile_expert, n_used, x_sorted, w_gu, w_down)


def route(h, w_router, router_bias):
    logits = jnp.dot(h.astype(F32), w_router, precision=lax.Precision.HIGHEST)
    scores = jax.nn.sigmoid(logits)
    sel = scores + router_bias
    grp = sel.reshape(-1, N_EXPERT_GROUPS, N_EXPERTS // N_EXPERT_GROUPS)
    grp_score = jnp.sum(lax.top_k(grp, 2)[0], axis=-1)
    _, gidx = lax.top_k(grp_score, TOPK_GROUPS)
    gmask = jnp.any(gidx[..., :, None] == jnp.arange(N_EXPERT_GROUPS), axis=-2)
    emask = jnp.repeat(gmask, N_EXPERTS // N_EXPERT_GROUPS, axis=-1)
    _, eidx = lax.top_k(jnp.where(emask, sel, -jnp.inf), TOP_K)
    w = jnp.take_along_axis(scores, eidx, axis=-1)
    w = w / jnp.sum(w, axis=-1, keepdims=True) * ROUTED_SCALE
    return eidx, w


def dispatch_plan(eidx):
    n_assign = eidx.size
    r = n_assign + N_EXPERTS * MOE_TILE
    n_tiles = r // MOE_TILE
    eflat = eidx.reshape(n_assign).astype(jnp.int32)
    order = jnp.argsort(eflat, stable=True).astype(jnp.int32)
    inv = jnp.argsort(order).astype(jnp.int32)
    counts = jnp.sum((eflat[:, None] == jnp.arange(N_EXPERTS, dtype=jnp.int32)[None, :]).astype(jnp.int32), axis=0)
    tiles_per = (counts + MOE_TILE - 1) // MOE_TILE
    tile_end = jnp.cumsum(tiles_per)
    tile_start = tile_end - tiles_per
    grp_start = jnp.cumsum(counts) - counts
    pos = tile_start[eflat] * MOE_TILE + (inv - grp_start[eflat])
    n_used = tile_end[-1:]
    tile_ids = jnp.arange(n_tiles, dtype=jnp.int32)
    tile_expert = jnp.sum((tile_ids[:, None] >= tile_end[None, :]).astype(jnp.int32), axis=1)
    tile_expert = jnp.minimum(tile_expert, tile_expert[n_used[0] - 1])
    rows = jnp.arange(r, dtype=jnp.int32)
    row_expert = tile_expert[rows // MOE_TILE]
    rank = rows - tile_start[row_expert] * MOE_TILE
    valid = jnp.logical_and(rank < counts[row_expert], rows < n_used[0] * MOE_TILE)
    slot = jnp.clip(grp_start[row_expert] + rank, 0, n_assign - 1)
    row_token = jnp.where(valid, order[slot] // TOP_K, 0)
    return row_token, pos, tile_expert, n_used


def routed_experts(h, p, layer):
    t = h.shape[0]
    eidx, w = route(h, p['w_router'], p['router_bias'])
    row_token, pos, tile_expert, n_used = dispatch_plan(eidx)
    x_sorted = h.at[row_token].get(mode='promise_in_bounds')
    y = moe_grouped(x_sorted, tile_expert, n_used, p['w_exp_gu'], p['w_exp_down'], layer)
    yg = y.at[pos].get(mode='promise_in_bounds').reshape(t, TOP_K, D_MODEL).astype(F32)
    return jnp.sum(yg * w[..., None], axis=1)


def layernorm(x, g, b):
    mu = jnp.mean(x, axis=-1, keepdims=True)
    var = jnp.mean(jnp.square(x - mu), axis=-1, keepdims=True)
    return (x - mu) * lax.rsqrt(var + EPS) * g + b


def dwconv_centred(u, w, b):
    pad = w.shape[0] // 2
    y = lax.conv_general_dilated(u, w[:, None, :].astype(u.dtype), window_strides=(1,),
                                 padding=[(pad, pad)], dimension_numbers=('NWC', 'WIO', 'NWC'),
                                 feature_group_count=u.shape[-1])
    return y + b.astype(u.dtype)


def flip(t):
    return jnp.flip(t, axis=1)


def fourier_mix(u):
    bsz, l, _ = u.shape
    g = u.reshape(bsz, l, FNET_GROUPS, FNET_GW)
    y = jnp.fft.fft2(g, axes=(1, 3), norm='ortho').real
    return y.reshape(bsz, l, FNET_W)


def conformer_conv(u, p):
    a, gt = jnp.split(u, 2, axis=-1)
    v = a * jax.nn.sigmoid(gt)
    v = dwconv_centred(v, p['conf_conv_w'], p['conf_conv_b'])
    v = layernorm(v, p['conf_ln_g'], p['conf_ln_b'])
    return jax.nn.silu(v)


def segsum(a):
    t = a.shape[-1]
    rep = jnp.broadcast_to(a[..., :, None], a.shape + (t,))
    rep = jnp.where(jnp.tril(jnp.ones((t, t), bool), -1), rep, 0.0)
    cs = jnp.cumsum(rep, axis=-2)
    return jnp.where(jnp.tril(jnp.ones((t, t), bool)), cs, -jnp.inf)


def ssd_scan(x, dt, a, bm, cm, init):
    bsz, l, h, pdim = x.shape
    n = bm.shape[-1]
    nc = l // SSD_CHUNK
    xd = (x * dt[..., None]).reshape(bsz, nc, SSD_CHUNK, h, pdim)
    bc = bm.reshape(bsz, nc, SSD_CHUNK, h, n)
    ad = jnp.transpose((dt * a).reshape(bsz, nc, SSD_CHUNK, h), (0, 3, 1, 2))
    a_cs = jnp.cumsum(ad, axis=-1)
    decay_to_end = jnp.exp(a_cs[..., -1:] - a_cs)
    chunk_states = jnp.einsum('bclhn,bhcl,bclhp->bchpn', bc, decay_to_end, xd)
    states = jnp.concatenate([init[:, None], chunk_states], axis=1)
    chunk_decay = jnp.exp(segsum(jnp.pad(a_cs[..., -1], ((0, 0), (0, 0), (1, 0)))))
    states = jnp.einsum('bhzc,bchpn->bzhpn', chunk_decay, states)
    final = states[:, -1]
    if cm is None:
        return None, final
    cc = cm.reshape(bsz, nc, SSD_CHUNK, h, n)
    y_diag = jnp.einsum('bclhn,bcshn,bhcls,bcshp->bclhp', cc, bc, jnp.exp(segsum(ad)), xd)
    y_off = jnp.einsum('bclhn,bchpn,bhcl->bclhp', cc, states[:, :-1], jnp.exp(a_cs))
    return (y_diag + y_off).reshape(bsz, l, h, pdim), final


def ssd_prepare(xbc_raw, dt_raw, p):
    bsz, l, ch = xbc_raw.shape
    u = jax.nn.silu(dwconv_centred(xbc_raw, p['ssd_conv_w'][:, :ch], p['ssd_conv_b'][:ch]))
    xs = u[..., :SSD_W].reshape(bsz, l, SSD_HEADS, SSD_HD)

    def to_heads(t):
        return jnp.repeat(t.reshape(bsz, l, SSD_GROUPS, SSD_STATE), SSD_HEADS // SSD_GROUPS, axis=2)

    bs = to_heads(u[..., SSD_W:SSD_W + SSD_BC])
    cs = to_heads(u[..., SSD_W + SSD_BC:]) if ch > SSD_W + SSD_BC else None
    dt = jax.nn.softplus(dt_raw.reshape(bsz, l, 2, SSD_HEADS) + p['ssd_dt_bias'])
    return xs, bs, cs, dt


def ssd_finish(y, xs, z, p):
    bsz, l = z.shape[:2]
    y = (y + p['ssd_d'][:, None] * xs).reshape(bsz, l, SSD_W)
    yg = (y * jax.nn.silu(z)).reshape(bsz, l, SSD_GROUPS, SSD_W // SSD_GROUPS)
    yg = yg * lax.rsqrt(jnp.mean(yg * yg, axis=-1, keepdims=True) + EPS)
    return yg.reshape(bsz, l, SSD_W) * p['ssd_norm_g']


def _lat(a):
    return a[:N_LAT].reshape(BATCH, SEQ, a.shape[-1])


def _ctx(a):
    return a[N_LAT:N_ROWS].reshape(BATCH, CTX_LEN, a.shape[-1])


def _rows(lat, ctx=None):
    lat = lat.reshape(N_LAT, lat.shape[-1])
    if ctx is None:
        return lat
    return jnp.concatenate([lat, ctx.reshape(N_CTX, ctx.shape[-1])], axis=0)


def _merge_epilogue(dots, extras):
    acc = None
    for d, g in zip(dots, extras):
        term = jax.nn.sigmoid(g.astype(F32)) * d
        acc = term if acc is None else acc + term
    return acc


def _residual_epilogue(dots, extras):
    x, g = extras
    return x + g * dots[0]


def _ffn_out_epilogue(dots, extras):
    x, routed, g = extras
    return x + g * (dots[0] + routed)


def _glu_epilogue(dots, extras):
    gate, up = dots
    return gate * jax.nn.sigmoid(gate) * up


def _norm_kernel(x_ref, g_ref, sc_ref, sh_ref, o_ref):
    x = x_ref[...]
    y = x * lax.rsqrt(jnp.mean(x * x, axis=-1, keepdims=True) + EPS) * g_ref[...]
    o_ref[...] = (y * (1.0 + sc_ref[...]) + sh_ref[...]).astype(o_ref.dtype)


def rms_modulate(x, g, scale, shift, *, m, out_dtype=BF16, tm=256):
    d = x.shape[1]
    seg_spec = pl.BlockSpec((None, 1, d), lambda i: (_segment_of_row_tile(i, tm), 0, 0))
    return pl.pallas_call(
        _norm_kernel,
        name="rms_modulate",
        grid=(m // tm,),
        in_specs=[pl.BlockSpec((tm, d), lambda i: (i, 0)),
                  pl.BlockSpec((1, d), lambda i: (0, 0)),
                  seg_spec, seg_spec],
        out_specs=pl.BlockSpec((tm, d), lambda i: (i, 0)),
        out_shape=jax.ShapeDtypeStruct((m, d), out_dtype),
        compiler_params=pltpu.CompilerParams(
            dimension_semantics=("parallel",), vmem_limit_bytes=VMEM_LIMIT_BYTES),
    )(x, g.reshape(1, d), scale.reshape(-1, 1, d), shift.reshape(-1, 1, d))


def token_mixers(h, p, layer_idx, ctx_out, m_full):
    w_in = p['w_in']
    proj_a = project(h, w_in, SDT_OFF, m=N_ROWS, prefix=(layer_idx,), name="proj_state")
    w_dt = jnp.pad(w_in[layer_idx, :, SDT_OFF:SC_OFF], ((0, 0), (0, LANE - 2 * SSD_HEADS)))
    proj_dt = project(h, w_dt, LANE, m=N_ROWS, out_dtype=F32, tn=LANE, name="proj_dt")[:, :2 * SSD_HEADS]
    proj_b = project(h, w_in[layer_idx, :, SC_OFF:GATE_OFF], GATE_OFF - SC_OFF, m=m_full,
                     name="proj_mix")
    proj_g = project(h, w_in[layer_idx, :, GATE_OFF:], N_BRANCH * D_MODEL, m=m_full, name="proj_gate")

    lam_init = 0.8 - 0.6 * math.exp(-0.3 * layer_idx)
    lam = (jnp.exp(jnp.sum(p['lam_q1'] * p['lam_k1'])) - jnp.exp(jnp.sum(p['lam_q2'] * p['lam_k2'])) + lam_init)
    cos_l, sin_l = rope_tables(SEQ)
    cos_c, sin_c = jnp.ones((CTX_LEN, LANE), F32), jnp.zeros((CTX_LEN, LANE), F32)
    kv_lat = _lat(proj_a[:, :SX_OFF])
    kv_ctx = _ctx(proj_a[:, :SX_OFF])
    kv_all = jnp.concatenate([kv_lat, kv_ctx], axis=1)
    rope_k = (jnp.concatenate([cos_l, cos_c]), jnp.concatenate([sin_l, sin_c]))
    y_diff = diff_attention(_lat(proj_b), kv_all, (cos_l, sin_l), rope_k, lam, p['g_subln'], lam_init,
                            q_col_block=B_Q // LANE)
    if ctx_out:
        yc_diff = diff_attention(_ctx(proj_b), kv_ctx, (cos_c, sin_c), (cos_c, sin_c), lam, p['g_subln'],
                                 lam_init, q_col_block=B_Q // LANE)

    pa = proj_a.astype(F32)
    pb = proj_b.astype(F32)
    a_ssd = -jnp.exp(p['ssd_a_log'])
    xbc_l = jnp.concatenate([_lat(pa[:, SX_OFF:]), _lat(pb[:, B_SC:B_Q])], axis=-1)
    xs, bs, cs, dts = ssd_prepare(xbc_l, _lat(proj_dt), p)
    xbc_c = (jnp.concatenate([_ctx(pa[:, SX_OFF:]), _ctx(pb[:, B_SC:B_Q])], axis=-1)
             if ctx_out else _ctx(pa[:, SX_OFF:]))
    xsc, bsc, csc, dtsc = ssd_prepare(xbc_c, _ctx(proj_dt), p)
    zero = jnp.zeros((BATCH, SSD_HEADS, SSD_HD, SSD_STATE), F32)
    yc_f, st_f = ssd_scan(xsc, dtsc[:, :, 0], a_ssd[0], bsc, csc, zero)
    yc_b, st_b = ssd_scan(flip(xsc), flip(dtsc[:, :, 1]), a_ssd[1], flip(bsc),
                          None if csc is None else flip(csc), zero)
    y_f, _ = ssd_scan(xs, dts[:, :, 0], a_ssd[0], bs, cs, st_f)
    y_b, _ = ssd_scan(flip(xs), flip(dts[:, :, 1]), a_ssd[1], flip(bs), flip(cs), st_b)
    y_ssd = ssd_finish(y_f + flip(y_b), xs, _lat(pb[:, B_SZ:B_FN]), p)

    y_fn = fourier_mix(_lat(pb[:, B_FN:B_CF]))
    y_cf = conformer_conv(_lat(pb[:, B_CF:]), p)

    if ctx_out:
        yc_ssd = ssd_finish(yc_f + flip(yc_b), xsc, _ctx(pb[:, B_SZ:B_FN]), p)
        yc_fn = fourier_mix(_ctx(pb[:, B_FN:B_CF]))
        yc_cf = conformer_conv(_ctx(pb[:, B_CF:]), p)
        branches = [_rows(y_fn, yc_fn), _rows(y_diff, yc_diff), _rows(y_ssd, yc_ssd), _rows(y_cf, yc_cf)]
    else:
        branches = [_rows(y_fn), _rows(y_diff), _rows(y_ssd), _rows(y_cf)]
    return [b.astype(BF16) for b in branches], proj_g


def trunk_layer(x, mods, p, layer_idx, ctx_out):
    sh1, sc1, g1, sh2, sc2, g2 = jnp.split(mods, 6, axis=-1)
    m_full = N_ROWS if ctx_out else N_LAT
    lp = (layer_idx,)
    h = rms_modulate(x, p['g_mix'], sc1, sh1, m=N_ROWS)
    branches, proj_g = token_mixers(h, p, layer_idx, ctx_out, m_full)

    gate_blocks = D_MODEL // COL_TILE
    acc = ws_matmul(
        branches, [(p['w_branch'], (layer_idx, i), lambda j: j) for i in range(N_BRANCH)], list(range(N_BRANCH)),
        _merge_epilogue,
        extras=[('tile', proj_g, lambda j, i=i: i * gate_blocks + j) for i in range(N_BRANCH)],
        m=m_full, n=D_MODEL, out_dtype=BF16, name="merge_branches")
    x = ws_matmul([acc], [(p['w_out'], lp, lambda j: j)], [0], _residual_epilogue,
                  extras=[('tile', x, lambda j: j), ('seg', g1.reshape(3, 1, D_MODEL))],
                  m=m_full, n=D_MODEL, out_dtype=F32, name="mix_out")

    h2 = rms_modulate(x, p['g_ffn'], sc2, sh2, m=m_full)
    routed = routed_experts(h2, p, layer_idx)
    up_blocks = SHARED_W // COL_TILE
    act = ws_matmul([h2], [(p['w_sh_gu'], lp, lambda j: j), (p['w_sh_gu'], lp, lambda j: j + up_blocks)], [0, 0],
                    _glu_epilogue, m=m_full, n=SHARED_W, out_dtype=BF16, name="shared_glu")
    x = ws_matmul([act], [(p['w_sh_down'], lp, lambda j: j)], [0], _ffn_out_epilogue,
                  extras=[('tile', x, lambda j: j), ('tile', routed, lambda j: j),
                          ('seg', g2.reshape(3, 1, D_MODEL))],
                  m=m_full, n=D_MODEL, out_dtype=F32, name="ffn_out")
    return x


def kernel(x, c, ctx, c_ctx, w_ada, b_ada, g_mix, w_in, lam_q1, lam_k1, lam_q2, lam_k2, g_subln, ssd_conv_w, ssd_conv_b, ssd_dt_bias, ssd_a_log, ssd_d, ssd_norm_g, conf_conv_w, conf_conv_b, conf_ln_g, conf_ln_b, w_branch, w_out, g_ffn, w_router, router_bias, w_exp_gu, w_exp_down, w_sh_gu, w_sh_down, g_final):
    cc = jnp.concatenate([jax.nn.silu(c), jax.nn.silu(c_ctx)[None]], axis=0)
    cc = jnp.pad(cc, ((0, 16 - cc.shape[0]), (0, 0))).astype(BF16)
    mods = ada_modulation(cc, w_ada, b_ada)[:, :BATCH + 1]

    xa = jnp.concatenate([x.reshape(N_LAT, D_MODEL), ctx.reshape(N_CTX, D_MODEL)], axis=0)
    for l in range(DEPTH):
        p = {
            'g_mix': g_mix[l], 'w_in': w_in,
            'lam_q1': lam_q1[l], 'lam_k1': lam_k1[l], 'lam_q2': lam_q2[l], 'lam_k2': lam_k2[l],
            'g_subln': g_subln[l], 'ssd_conv_w': ssd_conv_w[l], 'ssd_conv_b': ssd_conv_b[l],
            'ssd_dt_bias': ssd_dt_bias[l], 'ssd_a_log': ssd_a_log[l], 'ssd_d': ssd_d[l],
            'ssd_norm_g': ssd_norm_g[l], 'conf_conv_w': conf_conv_w[l], 'conf_conv_b': conf_conv_b[l],
            'conf_ln_g': conf_ln_g[l], 'conf_ln_b': conf_ln_b[l], 'w_branch': w_branch,
            'w_out': w_out, 'g_ffn': g_ffn[l], 'w_router': w_router[l], 'router_bias': router_bias[l],
            'w_exp_gu': w_exp_gu, 'w_exp_down': w_exp_down, 'w_sh_gu': w_sh_gu,
            'w_sh_down': w_sh_down,
        }
        xa = trunk_layer(xa, mods[l], p, l, l < DEPTH - 1)
    zeros = jnp.zeros((BATCH + 1, D_MODEL), F32)
    y = rms_modulate(xa, g_final, zeros, zeros, m=N_LAT, out_dtype=F32)
    return y.reshape(BATCH, SEQ, D_MODEL)
```

```python
import functools
import math

import jax
import jax.numpy as jnp
from jax import lax
from jax.experimental import pallas as pl
from jax.experimental.pallas import tpu as pltpu

F32 = jnp.float32
BF16 = jnp.bfloat16

D_MODEL = 4096
BATCH = 2
SEQ = 4096
DEPTH = 2
CTX_LEN = 256
GRID_W = 64
N_BRANCH = 4
EPS = 1e-6
FNET_GROUPS = 4
FNET_GW = 256
FNET_W = FNET_GROUPS * FNET_GW
DIFF_HEADS = 8
DIFF_DK = 64
DIFF_DV = 128
DIFF_QW = DIFF_HEADS * 2 * DIFF_DK
DIFF_W = DIFF_HEADS * DIFF_DV
DIFF_SCALE = DIFF_DK ** -0.5
ROPE_BASE = 10000.0
ROPE_FREQS = DIFF_DK // 4
SSD_HEADS = 16
SSD_HD = 64
SSD_W = SSD_HEADS * SSD_HD
SSD_STATE = 128
SSD_GROUPS = 4
SSD_BC = SSD_GROUPS * SSD_STATE
SSD_CONV = 5
SSD_CHUNK = 128
CONF_W = 1024
CONF_K = 31
N_EXPERTS = 64
TOP_K = 6
N_EXPERT_GROUPS = 8
TOPK_GROUPS = 4
EXPERT_W = 256
SHARED_W = 1024
ROUTED_SCALE = 2.5

K_OFF = 0
V_OFF = K_OFF + DIFF_QW
SX_OFF = V_OFF + DIFF_W
SB_OFF = SX_OFF + SSD_W
SDT_OFF = SB_OFF + SSD_BC
SC_OFF = SDT_OFF + 2 * SSD_HEADS
Q_OFF = SC_OFF + SSD_BC
SZ_OFF = Q_OFF + DIFF_QW
FN_OFF = SZ_OFF + SSD_W
CF_OFF = FN_OFF + FNET_W
GATE_OFF = CF_OFF + 2 * CONF_W
N_IN = GATE_OFF + N_BRANCH * D_MODEL

M_CF = 0
M_Q = M_CF + 2 * CONF_W
M_SZ = M_Q + DIFF_QW
M_FN = M_SZ + SSD_W
M_SC = M_FN + FNET_W

N_LAT = BATCH * SEQ
N_CTX = BATCH * CTX_LEN
N_ROWS = N_LAT + N_CTX
N_KEYS = SEQ + CTX_LEN

VMEM_LIMIT_BYTES = 56 * 1024 * 1024
LANE = 128
ROW_TILE = 512
COL_TILE = 512
ATTN_Q_TILE = 256
MOE_TILE = 256
MOE_ROWS_MAX = None


def _segment_of_row_tile(i, tm):
    return jnp.minimum(i * tm // SEQ, BATCH)


def _ws_kernel(*refs, n_a, a_of_w, n_extra, epilogue):
    n_w = len(a_of_w)
    a_refs = refs[:n_a]
    w_refs = refs[n_a:n_a + n_w]
    extra_refs = refs[n_a + n_w:n_a + n_w + n_extra]
    o_ref = refs[n_a + n_w + n_extra]
    wb_refs = refs[n_a + n_w + n_extra + 1:]

    @pl.when(pl.program_id(1) == 0)
    def _():
        for w_ref, wb_ref in zip(w_refs, wb_refs):
            wb_ref[...] = w_ref[...].astype(BF16)

    a_vals = [a_ref[...] for a_ref in a_refs]
    dots = [jnp.dot(a_vals[ai], wb_ref[...], preferred_element_type=F32)
            for ai, wb_ref in zip(a_of_w, wb_refs)]
    o_ref[...] = epilogue(dots, [e[...] for e in extra_refs]).astype(o_ref.dtype)


def ws_matmul(a_list, w_list, a_of_w, epilogue, extras=(), *, m, n, out_dtype, name, tm=ROW_TILE, tn=COL_TILE):
    assert m % tm == 0 and n % tn == 0
    in_specs = []
    operands = []
    for a in a_list:
        if isinstance(a, tuple):
            a, width, cb = a
            in_specs.append(pl.BlockSpec((tm, width), lambda j, i, cb=cb: (i, cb)))
        else:
            in_specs.append(pl.BlockSpec((tm, a.shape[1]), lambda j, i: (i, 0)))
        operands.append(a)
    scratch = []
    for w, prefix, colfn in w_list:
        kdim = w.shape[-2]
        in_specs.append(pl.BlockSpec((None,) * len(prefix) + (kdim, tn),
                                     lambda j, i, prefix=prefix, colfn=colfn: prefix + (0, colfn(j))))
        operands.append(w)
        scratch.append(pltpu.VMEM((kdim, tn), BF16))
    for ex in extras:
        if ex[0] == 'tile':
            _, arr, colfn = ex
            in_specs.append(pl.BlockSpec((tm, tn), lambda j, i, colfn=colfn: (i, colfn(j))))
        else:
            _, arr = ex
            in_specs.append(pl.BlockSpec((None, 1, tn), lambda j, i: (_segment_of_row_tile(i, tm), 0, j)))
        operands.append(arr)
    kern = functools.partial(_ws_kernel, n_a=len(a_list), a_of_w=tuple(a_of_w), n_extra=len(extras),
                             epilogue=epilogue)
    return pl.pallas_call(
        kern,
        name=name,
        grid=(n // tn, m // tm),
        in_specs=in_specs,
        out_specs=pl.BlockSpec((tm, tn), lambda j, i: (i, j)),
        out_shape=jax.ShapeDtypeStruct((m, n), out_dtype),
        scratch_shapes=scratch,
        compiler_params=pltpu.CompilerParams(
            dimension_semantics=("parallel", "arbitrary"), vmem_limit_bytes=VMEM_LIMIT_BYTES),
    )(*operands)


def _ident(dots, extras):
    return dots[0]


def project(h, w, n, *, m, name, prefix=(), out_dtype=BF16, tn=COL_TILE):
    return ws_matmul([h], [(w, prefix, lambda j: j)], [0], _ident, m=m, n=n, out_dtype=out_dtype, tn=tn,
                     name=name)


def _ada_kernel(c_ref, w_ref, b_ref, o_ref):
    o_ref[...] = jnp.dot(c_ref[...], w_ref[...].astype(BF16), preferred_element_type=F32) + b_ref[...]


def ada_modulation(cc, w_ada, b_ada, tn=COL_TILE):
    nl, d, n = w_ada.shape
    return pl.pallas_call(
        _ada_kernel,
        name="ada_modulation",
        grid=(nl, n // tn),
        in_specs=[pl.BlockSpec((16, d), lambda l, j: (0, 0)),
                  pl.BlockSpec((None, d, tn), lambda l, j: (l, 0, j)),
                  pl.BlockSpec((None, 1, tn), lambda l, j: (l, 0, j))],
        out_specs=pl.BlockSpec((None, 16, tn), lambda l, j: (l, 0, j)),
        out_shape=jax.ShapeDtypeStruct((nl, 16, n), F32),
        compiler_params=pltpu.CompilerParams(
            dimension_semantics=("parallel", "parallel"), vmem_limit_bytes=VMEM_LIMIT_BYTES),
    )(cc, w_ada, b_ada.reshape(nl, 1, n))


def _rope(u, cos, sin_signed, first_half):
    partner = jnp.where(first_half, pltpu.roll(u, LANE - ROPE_FREQS, axis=1), pltpu.roll(u, ROPE_FREQS, axis=1))
    return u * cos + partner * sin_signed


def _attn_kernel(lam_ref, q_ref, k_ref, v_ref, cq_ref, sq_ref, ck_ref, sk_ref, g_ref, o_ref, k_s,
                 *, out_scale):
    lane = lax.broadcasted_iota(jnp.int32, (1, LANE), 1)
    first_half = (lane % (2 * ROPE_FREQS)) < ROPE_FREQS
    comp0 = lane < DIFF_DK

    @pl.when(pl.program_id(2) == 0)
    def _():
        k_s[...] = _rope(k_ref[...].astype(F32), ck_ref[...], sk_ref[...], first_half).astype(BF16)

    q = _rope(q_ref[...].astype(F32), cq_ref[...], sq_ref[...], first_half) * DIFF_SCALE
    k = k_s[...]
    lam = lam_ref[0]
    probs = []
    for c in range(2):
        qc = jnp.where(comp0 if c == 0 else jnp.logical_not(comp0), q, 0.0).astype(BF16)
        s = lax.dot_general(qc, k, (((1,), (1,)), ((), ())), preferred_element_type=F32)
        s = s - jnp.max(s, axis=-1, keepdims=True)
        p = jnp.exp(s)
        r = 1.0 / jnp.sum(p, axis=-1, keepdims=True)
        probs.append(p * (r if c == 0 else r * lam))
    a = (probs[0] - probs[1]).astype(BF16)
    o = jnp.dot(a, v_ref[...], preferred_element_type=F32)
    o = o * lax.rsqrt(jnp.mean(o * o, axis=-1, keepdims=True) + EPS) * (g_ref[...] * out_scale)
    o_ref[...] = o.astype(o_ref.dtype)


def diff_attention(q, kv_all, rope_q, rope_k, lam, g_subln, lam_init, *, q_col_block):
    bsz, lq = q.shape[:2]
    keys = kv_all.shape[1]
    tq = min(ATTN_Q_TILE, lq)
    cq, sq = rope_q
    ck, sk = rope_k
    const = dict(pipeline_mode=pl.Buffered(1))
    kern = functools.partial(_attn_kernel, out_scale=1.0 - lam_init)
    return pl.pallas_call(
        kern,
        name="diff_attention",
        grid=(bsz, DIFF_HEADS, lq // tq),
        in_specs=[pl.BlockSpec(memory_space=pltpu.SMEM),
                  pl.BlockSpec((None, tq, LANE), lambda b, h, i: (b, i, q_col_block + h)),
                  pl.BlockSpec((None, keys, LANE), lambda b, h, i: (b, 0, h)),
                  pl.BlockSpec((None, keys, LANE), lambda b, h, i: (b, 0, DIFF_HEADS + h)),
                  pl.BlockSpec((tq, LANE), lambda b, h, i: (i, 0)),
                  pl.BlockSpec((tq, LANE), lambda b, h, i: (i, 0)),
                  pl.BlockSpec((keys, LANE), lambda b, h, i: (0, 0), **const),
                  pl.BlockSpec((keys, LANE), lambda b, h, i: (0, 0), **const),
                  pl.BlockSpec((1, LANE), lambda b, h, i: (0, 0))],
        out_specs=pl.BlockSpec((None, tq, LANE), lambda b, h, i: (b, i, h)),
        out_shape=jax.ShapeDtypeStruct((bsz, lq, DIFF_W), BF16),
        scratch_shapes=[pltpu.VMEM((keys, LANE), BF16)],
        compiler_params=pltpu.CompilerParams(
            dimension_semantics=("parallel", "parallel", "arbitrary"), vmem_limit_bytes=VMEM_LIMIT_BYTES),
    )(lam.reshape(1).astype(F32), q, kv_all, kv_all, cq, sq, ck, sk, g_subln.reshape(1, LANE).astype(F32))


def rope_tables(s):
    pos = jnp.arange(s, dtype=jnp.int32)
    row = (pos // GRID_W).astype(F32)[:, None]
    col = (pos % GRID_W).astype(F32)[:, None]
    lane = jnp.arange(LANE, dtype=jnp.int32)
    d = lane % DIFF_DK
    inv = ROPE_BASE ** (-(d % ROPE_FREQS).astype(F32) / ROPE_FREQS)
    ang = jnp.where((d < DIFF_DK // 2)[None, :], row * inv[None, :], col * inv[None, :])
    sign = jnp.where((d % (2 * ROPE_FREQS)) < ROPE_FREQS, -1.0, 1.0)
    return jnp.cos(ang), jnp.sin(ang) * sign[None, :]


HALF_D = D_MODEL // 2
HI_MASK = 0xFFFF0000
ROW_BITS = 14
assert N_ROWS <= 1 << ROW_BITS


def _pack_halves(y):
    n = y.shape[1] // 2
    lo = lax.bitcast_convert_type(y[:, :n].astype(BF16).astype(F32), jnp.uint32)
    hi = lax.bitcast_convert_type(y[:, n:].astype(BF16).astype(F32), jnp.uint32)
    return (lo >> 16) | (hi & jnp.uint32(HI_MASK))


def _unpack_halves(w):
    lo = lax.bitcast_convert_type(w << 16, F32)
    hi = lax.bitcast_convert_type(w & jnp.uint32(HI_MASK), F32)
    return lo, hi


def _moe_kernel(te_ref, nu_ref, tv_ref, plan_ref, x_hbm, wgu_ref, wdn_ref, y_hbm,
                xbuf, ybuf, pad_s, wgu_s, wdn_s, gsem, ssem):
    t = pl.program_id(0)
    n_used = nu_ref[0]
    slot = t % 2
    active = t < n_used

    def gather_rows(tile, dst_slot):
        def body(r, carry):
            src = plan_ref[tile * MOE_TILE + r] & ((1 << ROW_BITS) - 1)
            pltpu.make_async_copy(x_hbm.at[pl.ds(src, 1)], xbuf.at[dst_slot, pl.ds(r, 1)],
                                  gsem.at[dst_slot]).start()
            return carry
        lax.fori_loop(0, MOE_TILE, body, 0, unroll=8)

    def scatter_rows(tile, src_slot):
        n_valid = tv_ref[tile]

        def body(r, carry):
            dst = plan_ref[tile * MOE_TILE + r] >> ROW_BITS
            pltpu.make_async_copy(ybuf.at[src_slot, pl.ds(r, 1)], y_hbm.at[pl.ds(dst, 1)],
                                  ssem.at[src_slot]).start()
            return carry
        lax.fori_loop(0, n_valid, body, 0)

        def pad_body(r, carry):
            pltpu.make_async_copy(ybuf.at[src_slot, pl.ds(r, 1)], pad_s.at[src_slot, pl.ds(r, 1)],
                                  ssem.at[src_slot]).start()
            return carry
        lax.fori_loop(n_valid, MOE_TILE, pad_body, 0)

    def wait_gather(s):
        pltpu.make_async_copy(x_hbm.at[pl.ds(0, MOE_TILE)], xbuf.at[s], gsem.at[s]).wait()

    def wait_scatter(s):
        pltpu.make_async_copy(ybuf.at[s], y_hbm.at[pl.ds(0, MOE_TILE)], ssem.at[s]).wait()

    @pl.when(t == 0)
    def _():
        gather_rows(0, 0)

    @pl.when(t + 1 < n_used)
    def _():
        gather_rows(t + 1, 1 - slot)

    @pl.when(jnp.logical_and(t >= 2, t - 2 < n_used))
    def _():
        wait_scatter(slot)

    prev = te_ref[jnp.maximum(t - 1, 0)]

    @pl.when(jnp.logical_or(t == 0, te_ref[t] != prev))
    def _():
        wgu_s[...] = wgu_ref[...].astype(BF16)
        wdn_s[...] = wdn_ref[...].astype(BF16)

    @pl.when(active)
    def _():
        wait_gather(slot)
        lo, hi = _unpack_halves(xbuf[slot])
        gu = (jnp.dot(lo.astype(BF16), wgu_s[:HALF_D], preferred_element_type=F32)
              + jnp.dot(hi.astype(BF16), wgu_s[HALF_D:], preferred_element_type=F32))
        gate = gu[:, :EXPERT_W]
        act = (gate * jax.nn.sigmoid(gate) * gu[:, EXPERT_W:]).astype(BF16)
        ybuf[slot] = _pack_halves(jnp.dot(act, wdn_s[...], preferred_element_type=F32))
        scatter_rows(t, slot)

    @pl.when(t == pl.num_programs(0) - 1)
    def _():
        @pl.when(jnp.logical_and(t >= 1, t - 1 < n_used))
        def _():
            wait_scatter(1 - slot)

        @pl.when(active)
        def _():
            wait_scatter(slot)


def moe_grouped(x_packed, tile_expert, n_used, tile_valid, plan, w_gu, w_down, layer):
    r = plan.shape[0]
    n_tiles = r // MOE_TILE
    n_out = x_packed.shape[0] * TOP_K
    return pl.pallas_call(
        _moe_kernel,
        name="moe_experts",
        grid_spec=pltpu.PrefetchScalarGridSpec(
            num_scalar_prefetch=4,
            grid=(n_tiles,),
            in_specs=[pl.BlockSpec(memory_space=pl.ANY),
                      pl.BlockSpec((None, None, D_MODEL, 2 * EXPERT_W),
                                   lambda t, te, nu, tv, plan: (layer, te[t], 0, 0)),
                      pl.BlockSpec((None, None, EXPERT_W, D_MODEL),
                                   lambda t, te, nu, tv, plan: (layer, te[t], 0, 0))],
            out_specs=pl.BlockSpec(memory_space=pl.ANY),
            scratch_shapes=[pltpu.VMEM((2, MOE_TILE, HALF_D), jnp.uint32),
                            pltpu.VMEM((2, MOE_TILE, HALF_D), jnp.uint32),
                            pltpu.VMEM((2, MOE_TILE, HALF_D), jnp.uint32),
                            pltpu.VMEM((D_MODEL, 2 * EXPERT_W), BF16),
                            pltpu.VMEM((EXPERT_W, D_MODEL), BF16),
                            pltpu.SemaphoreType.DMA((2,)),
                            pltpu.SemaphoreType.DMA((2,))]),
        out_shape=jax.ShapeDtypeStruct((n_out, HALF_D), jnp.uint32),
        compiler_params=pltpu.CompilerParams(
            dimension_semantics=("arbitrary",), vmem_limit_bytes=VMEM_LIMIT_BYTES),
    )(tile_expert, n_used, tile_valid, plan, x_packed, w_gu, w_down)


def _combine_kernel(w_ref, *refs):
    y_refs, o_ref = refs[:TOP_K], refs[TOP_K]
    w = w_ref[...]
    acc_lo = acc_hi = None
    for k in range(TOP_K):
        lo, hi = _unpack_halves(y_refs[k][...])
        wk = w[:, k:k + 1]
        acc_lo = lo * wk if acc_lo is None else acc_lo + lo * wk
        acc_hi = hi * wk if acc_hi is None else acc_hi + hi * wk
    o_ref[:, :HALF_D] = acc_lo.astype(o_ref.dtype)
    o_ref[:, HALF_D:] = acc_hi.astype(o_ref.dtype)


def moe_combine(y, w, t, tm=256):
    blocks = t // tm
    return pl.pallas_call(
        _combine_kernel,
        name="moe_combine",
        grid=(blocks,),
        in_specs=[pl.BlockSpec((tm, TOP_K), lambda i: (i, 0))]
                 + [pl.BlockSpec((tm, HALF_D), lambda i, k=k: (k * blocks + i, 0)) for k in range(TOP_K)],
        out_specs=pl.BlockSpec((tm, D_MODEL), lambda i: (i, 0)),
        out_shape=jax.ShapeDtypeStruct((t, D_MODEL), BF16),
        compiler_params=pltpu.CompilerParams(
            dimension_semantics=("parallel",), vmem_limit_bytes=VMEM_LIMIT_BYTES),
    )(w, *([y] * TOP_K))


def route(h, w_router, router_bias):
    logits = jnp.dot(h.astype(F32), w_router, precision=lax.Precision.HIGHEST)
    scores = jax.nn.sigmoid(logits)
    sel = scores + router_bias
    grp = sel.reshape(-1, N_EXPERT_GROUPS, N_EXPERTS // N_EXPERT_GROUPS)
    grp_score = jnp.sum(lax.top_k(grp, 2)[0], axis=-1)
    _, gidx = lax.top_k(grp_score, TOPK_GROUPS)
    gmask = jnp.any(gidx[..., :, None] == jnp.arange(N_EXPERT_GROUPS), axis=-2)
    emask = jnp.repeat(gmask, N_EXPERTS // N_EXPERT_GROUPS, axis=-1)
    _, eidx = lax.top_k(jnp.where(emask, sel, -jnp.inf), TOP_K)
    w = jnp.take_along_axis(scores, eidx, axis=-1)
    w = w / jnp.sum(w, axis=-1, keepdims=True) * ROUTED_SCALE
    return eidx, w


def dispatch_plan(eidx):
    t = eidx.shape[0]
    n_assign = eidx.size
    r = n_assign + N_EXPERTS * MOE_TILE
    n_tiles = r // MOE_TILE
    eflat = eidx.reshape(n_assign).astype(jnp.int32)
    order = jnp.argsort(eflat, stable=True).astype(jnp.int32)
    counts = jnp.sum((eflat[:, None] == jnp.arange(N_EXPERTS, dtype=jnp.int32)[None, :]).astype(jnp.int32), axis=0)
    tiles_per = (counts + MOE_TILE - 1) // MOE_TILE
    tile_end = jnp.cumsum(tiles_per)
    tile_start = tile_end - tiles_per
    grp_start = jnp.cumsum(counts) - counts
    n_used = tile_end[-1:]
    tile_ids = jnp.arange(n_tiles, dtype=jnp.int32)
    tile_expert = jnp.sum((tile_ids[:, None] >= tile_end[None, :]).astype(jnp.int32), axis=1)
    tile_expert = jnp.minimum(tile_expert, tile_expert[n_used[0] - 1])
    rows = jnp.arange(r, dtype=jnp.int32)
    row_expert = tile_expert[rows // MOE_TILE]
    rank = rows - tile_start[row_expert] * MOE_TILE
    valid = jnp.logical_and(rank < counts[row_expert], rows < n_used[0] * MOE_TILE)
    slot = jnp.clip(grp_start[row_expert] + rank, 0, n_assign - 1)
    assign = order[slot]
    src = jnp.where(valid, assign // TOP_K, rows % t)
    dst = jnp.where(valid, (assign % TOP_K) * t + assign // TOP_K, 0)
    plan = (dst << ROW_BITS) | src
    tile_valid = jnp.clip(counts[tile_expert] - (tile_ids - tile_start[tile_expert]) * MOE_TILE, 0, MOE_TILE)
    tile_valid = jnp.where(tile_ids < n_used[0], tile_valid, 0)
    return plan, tile_expert, tile_valid, n_used


def routed_experts(h, h_packed, p, layer):
    t = h.shape[0]
    eidx, w = route(h, p['w_router'], p['router_bias'])
    plan, tile_expert, tile_valid, n_used = dispatch_plan(eidx)
    y = moe_grouped(h_packed, tile_expert, n_used, tile_valid, plan, p['w_exp_gu'], p['w_exp_down'], layer)
    return moe_combine(y, w, t)


def layernorm(x, g, b):
    mu = jnp.mean(x, axis=-1, keepdims=True)
    var = jnp.mean(jnp.square(x - mu), axis=-1, keepdims=True)
    return (x - mu) * lax.rsqrt(var + EPS) * g + b


def dwconv_centred(u, w, b):
    pad = w.shape[0] // 2
    y = lax.conv_general_dilated(u, w[:, None, :].astype(u.dtype), window_strides=(1,),
                                 padding=[(pad, pad)], dimension_numbers=('NWC', 'WIO', 'NWC'),
                                 feature_group_count=u.shape[-1])
    return y + b.astype(u.dtype)


def flip(t):
    return jnp.flip(t, axis=1)


def dft_tables(l, scale):
    f = 1
    while f * f < l:
        f *= 2
    assert l % f == 0
    k = jnp.arange(l, dtype=jnp.int32)[None, :]
    j1 = jnp.arange(l // f, dtype=jnp.int32)[:, None]
    j2 = jnp.arange(f, dtype=jnp.int32)[:, None]
    alpha = (2.0 * math.pi / l) * ((j1 * f * k) % l).astype(F32)
    beta = (2.0 * math.pi / l) * ((j2 * k) % l).astype(F32)
    ca, sa, cb, sb = jnp.cos(alpha)[:, None], jnp.sin(alpha)[:, None], jnp.cos(beta)[None], jnp.sin(beta)[None]
    cos = (ca * cb - sa * sb).reshape(l, l) * scale
    sin = (sa * cb + ca * sb).reshape(l, l) * scale
    return cos, sin


def channel_dft_weight():
    cos, sin = dft_tables(FNET_GW, 1.0)
    eye = jnp.eye(FNET_GROUPS, dtype=F32)
    return jnp.concatenate([jnp.kron(eye, cos), jnp.kron(eye, sin)], axis=1)


def _dft_seq_kernel(c_ref, s_ref, p_ref, q_ref, o_ref):
    o_ref[...] = (jnp.dot(c_ref[...], p_ref[...], preferred_element_type=F32)
                  - jnp.dot(s_ref[...], q_ref[...], preferred_element_type=F32)).astype(o_ref.dtype)


def fourier_mix(pq, tables, *, row_start, l, n_seq, tm=ROW_TILE, tn=COL_TILE):
    cos, sin = tables
    tm = min(tm, l)
    seq0 = row_start // l
    nj = FNET_W // tn
    return pl.pallas_call(
        _dft_seq_kernel,
        name="fourier_seq",
        grid=(n_seq, nj, l // tm),
        in_specs=[pl.BlockSpec((tm, l), lambda b, j, i: (i, 0)),
                  pl.BlockSpec((tm, l), lambda b, j, i: (i, 0)),
                  pl.BlockSpec((l, tn), lambda b, j, i: (seq0 + b, j)),
                  pl.BlockSpec((l, tn), lambda b, j, i: (seq0 + b, nj + j))],
        out_specs=pl.BlockSpec((tm, tn), lambda b, j, i: (b * (l // tm) + i, j)),
        out_shape=jax.ShapeDtypeStruct((n_seq * l, FNET_W), BF16),
        compiler_params=pltpu.CompilerParams(
            dimension_semantics=("parallel", "parallel", "arbitrary"), vmem_limit_bytes=VMEM_LIMIT_BYTES),
    )(cos, sin, pq, pq)


CONF_HALO = 16


CONF_ROWS = 64


def _conformer_kernel(prev_ref, a_ref, g_ref, next_ref, pg_ref, ng_ref, w_ref, b_ref, lg_ref, lb_ref,
                      o_ref, v_s, c_s, *, tl):
    i = pl.program_id(1)
    last = pl.num_programs(1) - 1

    def glu(a, g):
        a = a.astype(F32)
        return a * jax.nn.sigmoid(g.astype(F32))

    v_s[0:CONF_HALO] = jnp.where(i > 0, glu(prev_ref[...], pg_ref[...]), 0.0)
    v_s[CONF_HALO:CONF_HALO + tl] = glu(a_ref[...], g_ref[...])
    v_s[CONF_HALO + tl:] = jnp.where(i < last, glu(next_ref[...], ng_ref[...]), 0.0)
    for c in range(CONF_W // LANE):
        lanes = slice(c * LANE, (c + 1) * LANE)
        w = w_ref[:, lanes]
        for rb in range(tl // CONF_ROWS):
            acc = jnp.zeros((CONF_ROWS, LANE), F32) + b_ref[:, lanes]
            for k in range(CONF_K):
                off = rb * CONF_ROWS + CONF_HALO - CONF_K // 2 + k
                acc = acc + v_s[off:off + CONF_ROWS, lanes] * w[k:k + 1]
            c_s[rb * CONF_ROWS:(rb + 1) * CONF_ROWS, lanes] = acc
    acc = c_s[...]
    mu = jnp.mean(acc, axis=-1, keepdims=True)
    cen = acc - mu
    var = jnp.mean(cen * cen, axis=-1, keepdims=True)
    y = cen * lax.rsqrt(var + EPS) * lg_ref[...] + lb_ref[...]
    o_ref[...] = (y * jax.nn.sigmoid(y)).astype(o_ref.dtype)


def conformer_conv(proj, p, *, col_block, row_start, l, n_seq, tl=256):
    tl = min(tl, l)
    nt = l // tl
    hb = tl // CONF_HALO
    first = row_start // CONF_HALO

    def main(c):
        return pl.BlockSpec((tl, CONF_W), lambda b, i: (row_start // tl + b * nt + i, col_block + c))

    def prev(c):
        return pl.BlockSpec((CONF_HALO, CONF_W),
                            lambda b, i: (first + jnp.maximum((b * nt + i) * hb - 1, 0), col_block + c))

    def nxt(c):
        return pl.BlockSpec((CONF_HALO, CONF_W),
                            lambda b, i: (first + jnp.minimum((b * nt + i + 1) * hb, n_seq * nt * hb - 1),
                                          col_block + c))

    vec = pl.BlockSpec((1, CONF_W), lambda b, i: (0, 0))
    kern = functools.partial(_conformer_kernel, tl=tl)
    return pl.pallas_call(
        kern,
        name="conformer_conv",
        grid=(n_seq, nt),
        in_specs=[prev(0), main(0), main(1), nxt(0), prev(1), nxt(1),
                  pl.BlockSpec((CONF_K, CONF_W), lambda b, i: (0, 0)), vec, vec, vec],
        out_specs=pl.BlockSpec((tl, CONF_W), lambda b, i: (b * nt + i, 0)),
        out_shape=jax.ShapeDtypeStruct((n_seq * l, CONF_W), BF16),
        scratch_shapes=[pltpu.VMEM((tl + 2 * CONF_HALO, CONF_W), F32), pltpu.VMEM((tl, CONF_W), F32)],
        compiler_params=pltpu.CompilerParams(
            dimension_semantics=("parallel", "parallel"), vmem_limit_bytes=VMEM_LIMIT_BYTES),
    )(proj, proj, proj, proj, proj, proj, p['conf_conv_w'], p['conf_conv_b'].reshape(1, CONF_W),
      p['conf_ln_g'].reshape(1, CONF_W), p['conf_ln_b'].reshape(1, CONF_W))


def segsum(a):
    t = a.shape[-1]
    rep = jnp.broadcast_to(a[..., :, None], a.shape + (t,))
    rep = jnp.where(jnp.tril(jnp.ones((t, t), bool), -1), rep, 0.0)
    cs = jnp.cumsum(rep, axis=-2)
    return jnp.where(jnp.tril(jnp.ones((t, t), bool)), cs, -jnp.inf)


def ssd_scan(x, dt, a, bm, cm, init):
    bsz, l, h, pdim = x.shape
    n = bm.shape[-1]
    nc = l // SSD_CHUNK
    xd = (x * dt[..., None]).reshape(bsz, nc, SSD_CHUNK, h, pdim)
    bc = bm.reshape(bsz, nc, SSD_CHUNK, h, n)
    ad = jnp.transpose((dt * a).reshape(bsz, nc, SSD_CHUNK, h), (0, 3, 1, 2))
    a_cs = jnp.cumsum(ad, axis=-1)
    decay_to_end = jnp.exp(a_cs[..., -1:] - a_cs)
    chunk_states = jnp.einsum('bclhn,bhcl,bclhp->bchpn', bc, decay_to_end, xd)
    states = jnp.concatenate([init[:, None], chunk_states], axis=1)
    chunk_decay = jnp.exp(segsum(jnp.pad(a_cs[..., -1], ((0, 0), (0, 0), (1, 0)))))
    states = jnp.einsum('bhzc,bchpn->bzhpn', chunk_decay, states)
    final = states[:, -1]
    if cm is None:
        return None, final
    cc = cm.reshape(bsz, nc, SSD_CHUNK, h, n)
    y_diag = jnp.einsum('bclhn,bcshn,bhcls,bcshp->bclhp', cc, bc, jnp.exp(segsum(ad)), xd)
    y_off = jnp.einsum('bclhn,bchpn,bhcl->bclhp', cc, states[:, :-1], jnp.exp(a_cs))
    return (y_diag + y_off).reshape(bsz, l, h, pdim), final


def ssd_prepare(xbc_raw, dt_raw, p):
    bsz, l, ch = xbc_raw.shape
    u = jax.nn.silu(dwconv_centred(xbc_raw, p['ssd_conv_w'][:, :ch], p['ssd_conv_b'][:ch]))
    xs = u[..., :SSD_W].reshape(bsz, l, SSD_HEADS, SSD_HD)

    def to_heads(t):
        return jnp.repeat(t.reshape(bsz, l, SSD_GROUPS, SSD_STATE), SSD_HEADS // SSD_GROUPS, axis=2)

    bs = to_heads(u[..., SSD_W:SSD_W + SSD_BC])
    cs = to_heads(u[..., SSD_W + SSD_BC:]) if ch > SSD_W + SSD_BC else None
    dt = jax.nn.softplus(dt_raw.reshape(bsz, l, 2, SSD_HEADS) + p['ssd_dt_bias'])
    return xs, bs, cs, dt


def ssd_finish(y, xs, z, p):
    bsz, l = z.shape[:2]
    y = (y + p['ssd_d'][:, None] * xs).reshape(bsz, l, SSD_W)
    yg = (y * jax.nn.silu(z)).reshape(bsz, l, SSD_GROUPS, SSD_W // SSD_GROUPS)
    yg = yg * lax.rsqrt(jnp.mean(yg * yg, axis=-1, keepdims=True) + EPS)
    return yg.reshape(bsz, l, SSD_W) * p['ssd_norm_g']


def _lat(a):
    return a[:N_LAT].reshape(BATCH, SEQ, a.shape[-1])


def _ctx(a):
    return a[N_LAT:N_ROWS].reshape(BATCH, CTX_LEN, a.shape[-1])


def _rows(lat, ctx=None):
    lat = lat.reshape(N_LAT, lat.shape[-1])
    if ctx is None:
        return lat
    return jnp.concatenate([lat, ctx.reshape(N_CTX, ctx.shape[-1])], axis=0)


def _merge_epilogue(dots, extras):
    acc = None
    for d, g in zip(dots, extras):
        term = jax.nn.sigmoid(g.astype(F32)) * d
        acc = term if acc is None else acc + term
    return acc


def _residual_epilogue(dots, extras):
    x, g = extras
    return x + g * dots[0]


def _ffn_out_epilogue(dots, extras):
    x, routed, g = extras
    return x + g * (dots[0] + routed)


def _glu_epilogue(dots, extras):
    gate, up = dots
    return gate * jax.nn.sigmoid(gate) * up


def _norm_kernel(x_ref, g_ref, sc_ref, sh_ref, o_ref, *packed_ref):
    x = x_ref[...]
    y = x * lax.rsqrt(jnp.mean(x * x, axis=-1, keepdims=True) + EPS) * g_ref[...]
    y = y * (1.0 + sc_ref[...]) + sh_ref[...]
    o_ref[...] = y.astype(o_ref.dtype)
    if packed_ref:
        packed_ref[0][...] = _pack_halves(y)


def rms_modulate(x, g, scale, shift, *, m, out_dtype=BF16, packed=False, tm=256):
    d = x.shape[1]
    seg_spec = pl.BlockSpec((None, 1, d), lambda i: (_segment_of_row_tile(i, tm), 0, 0))
    out_specs = [pl.BlockSpec((tm, d), lambda i: (i, 0))]
    out_shape = [jax.ShapeDtypeStruct((m, d), out_dtype)]
    if packed:
        out_specs.append(pl.BlockSpec((tm, d // 2), lambda i: (i, 0)))
        out_shape.append(jax.ShapeDtypeStruct((m, d // 2), jnp.uint32))
    outs = pl.pallas_call(
        _norm_kernel,
        name="rms_modulate",
        grid=(m // tm,),
        in_specs=[pl.BlockSpec((tm, d), lambda i: (i, 0)),
                  pl.BlockSpec((1, d), lambda i: (0, 0)),
                  seg_spec, seg_spec],
        out_specs=out_specs,
        out_shape=out_shape,
        compiler_params=pltpu.CompilerParams(
            dimension_semantics=("parallel",), vmem_limit_bytes=VMEM_LIMIT_BYTES),
    )(x, g.reshape(1, d), scale.reshape(-1, 1, d), shift.reshape(-1, 1, d))
    return outs if packed else outs[0]


def token_mixers(h, p, layer_idx, ctx_out, m_full):
    w_in = p['w_in']
    proj_a = project(h, w_in, SDT_OFF, m=N_ROWS, prefix=(layer_idx,), name="proj_state")
    w_dt = jnp.pad(w_in[layer_idx, :, SDT_OFF:SC_OFF], ((0, 0), (0, LANE - 2 * SSD_HEADS)))
    proj_dt = project(h, w_dt, LANE, m=N_ROWS, out_dtype=F32, tn=LANE, name="proj_dt")[:, :2 * SSD_HEADS]
    wl = w_in[layer_idx]
    w_mix = jnp.concatenate([wl[:, CF_OFF:GATE_OFF], wl[:, Q_OFF:CF_OFF], wl[:, SC_OFF:Q_OFF]], axis=1)
    proj_b = project(h, w_mix, GATE_OFF - SC_OFF, m=m_full, name="proj_mix")
    proj_g = project(h, w_in[layer_idx, :, GATE_OFF:], N_BRANCH * D_MODEL, m=m_full, name="proj_gate")

    lam_init = 0.8 - 0.6 * math.exp(-0.3 * layer_idx)
    lam = (jnp.exp(jnp.sum(p['lam_q1'] * p['lam_k1'])) - jnp.exp(jnp.sum(p['lam_q2'] * p['lam_k2'])) + lam_init)
    cos_l, sin_l = rope_tables(SEQ)
    cos_c, sin_c = jnp.ones((CTX_LEN, LANE), F32), jnp.zeros((CTX_LEN, LANE), F32)
    kv_lat = _lat(proj_a[:, :SX_OFF])
    kv_ctx = _ctx(proj_a[:, :SX_OFF])
    kv_all = jnp.concatenate([kv_lat, kv_ctx], axis=1)
    rope_k = (jnp.concatenate([cos_l, cos_c]), jnp.concatenate([sin_l, sin_c]))
    y_diff = diff_attention(_lat(proj_b), kv_all, (cos_l, sin_l), rope_k, lam, p['g_subln'], lam_init,
                            q_col_block=M_Q // LANE)
    if ctx_out:
        yc_diff = diff_attention(_ctx(proj_b), kv_ctx, (cos_c, sin_c), (cos_c, sin_c), lam, p['g_subln'],
                                 lam_init, q_col_block=M_Q // LANE)

    pa = proj_a.astype(F32)
    pb = proj_b[:, M_SZ:M_FN].astype(F32)
    pc = proj_b[:, M_SC:].astype(F32)
    a_ssd = -jnp.exp(p['ssd_a_log'])
    xbc_l = jnp.concatenate([_lat(pa[:, SX_OFF:]), _lat(pc)], axis=-1)
    xs, bs, cs, dts = ssd_prepare(xbc_l, _lat(proj_dt), p)
    xbc_c = (jnp.concatenate([_ctx(pa[:, SX_OFF:]), _ctx(pc)], axis=-1)
             if ctx_out else _ctx(pa[:, SX_OFF:]))
    xsc, bsc, csc, dtsc = ssd_prepare(xbc_c, _ctx(proj_dt), p)
    zero = jnp.zeros((BATCH, SSD_HEADS, SSD_HD, SSD_STATE), F32)
    yc_f, st_f = ssd_scan(xsc, dtsc[:, :, 0], a_ssd[0], bsc, csc, zero)
    yc_b, st_b = ssd_scan(flip(xsc), flip(dtsc[:, :, 1]), a_ssd[1], flip(bsc),
                          None if csc is None else flip(csc), zero)
    y_f, _ = ssd_scan(xs, dts[:, :, 0], a_ssd[0], bs, cs, st_f)
    y_b, _ = ssd_scan(flip(xs), flip(dts[:, :, 1]), a_ssd[1], flip(bs), flip(cs), st_b)
    y_ssd = ssd_finish(y_f + flip(y_b), xs, _lat(pb), p)

    pq = ws_matmul([(proj_b, FNET_W, M_FN // FNET_W)], [(p['w_chan_dft'], (), lambda j: j)], [0], _ident,
                   m=m_full, n=2 * FNET_W, out_dtype=BF16, name="fourier_chan")
    y_fn = fourier_mix(pq, p['dft_lat'], row_start=0, l=SEQ, n_seq=BATCH)
    y_cf = conformer_conv(proj_b, p, col_block=M_CF // CONF_W, row_start=0, l=SEQ, n_seq=BATCH)

    if ctx_out:
        yc_ssd = ssd_finish(yc_f + flip(yc_b), xsc, _ctx(pb), p)
        yc_fn = fourier_mix(pq, p['dft_ctx'], row_start=N_LAT, l=CTX_LEN, n_seq=BATCH)
        yc_cf = conformer_conv(proj_b, p, col_block=M_CF // CONF_W, row_start=N_LAT, l=CTX_LEN, n_seq=BATCH)
        branches = [jnp.concatenate([y_fn, yc_fn]), _rows(y_diff, yc_diff), _rows(y_ssd, yc_ssd),
                    jnp.concatenate([y_cf, yc_cf])]
    else:
        branches = [y_fn, _rows(y_diff), _rows(y_ssd), y_cf]
    return [b.astype(BF16) for b in branches], proj_g


def trunk_layer(x, mods, p, layer_idx, ctx_out):
    sh1, sc1, g1, sh2, sc2, g2 = jnp.split(mods, 6, axis=-1)
    m_full = N_ROWS if ctx_out else N_LAT
    lp = (layer_idx,)
    h = rms_modulate(x, p['g_mix'], sc1, sh1, m=N_ROWS)
    branches, proj_g = token_mixers(h, p, layer_idx, ctx_out, m_full)

    gate_blocks = D_MODEL // COL_TILE
    acc = ws_matmul(
        branches, [(p['w_branch'], (layer_idx, i), lambda j: j) for i in range(N_BRANCH)], list(range(N_BRANCH)),
        _merge_epilogue,
        extras=[('tile', proj_g, lambda j, i=i: i * gate_blocks + j) for i in range(N_BRANCH)],
        m=m_full, n=D_MODEL, out_dtype=BF16, name="merge_branches")
    x = ws_matmul([acc], [(p['w_out'], lp, lambda j: j)], [0], _residual_epilogue,
                  extras=[('tile', x, lambda j: j), ('seg', g1.reshape(3, 1, D_MODEL))],
                  m=m_full, n=D_MODEL, out_dtype=F32, name="mix_out")

    h2, h2_packed = rms_modulate(x, p['g_ffn'], sc2, sh2, m=m_full, packed=True)
    routed = routed_experts(h2, h2_packed, p, layer_idx)
    up_blocks = SHARED_W // COL_TILE
    act = ws_matmul([h2], [(p['w_sh_gu'], lp, lambda j: j), (p['w_sh_gu'], lp, lambda j: j + up_blocks)], [0, 0],
                    _glu_epilogue, m=m_full, n=SHARED_W, out_dtype=BF16, name="shared_glu")
    x = ws_matmul([act], [(p['w_sh_down'], lp, lambda j: j)], [0], _ffn_out_epilogue,
                  extras=[('tile', x, lambda j: j), ('tile', routed, lambda j: j),
                          ('seg', g2.reshape(3, 1, D_MODEL))],
                  m=m_full, n=D_MODEL, out_dtype=F32, name="ffn_out")
    return x


def kernel(x, c, ctx, c_ctx, w_ada, b_ada, g_mix, w_in, lam_q1, lam_k1, lam_q2, lam_k2, g_subln, ssd_conv_w, ssd_conv_b, ssd_dt_bias, ssd_a_log, ssd_d, ssd_norm_g, conf_conv_w, conf_conv_b, conf_ln_g, conf_ln_b, w_branch, w_out, g_ffn, w_router, router_bias, w_exp_gu, w_exp_down, w_sh_gu, w_sh_down, g_final):
    cc = jnp.concatenate([jax.nn.silu(c), jax.nn.silu(c_ctx)[None]], axis=0)
    cc = jnp.pad(cc, ((0, 16 - cc.shape[0]), (0, 0))).astype(BF16)
    mods = ada_modulation(cc, w_ada, b_ada)[:, :BATCH + 1]

    dft_lat = tuple(t.astype(BF16) for t in dft_tables(SEQ, (SEQ * FNET_GW) ** -0.5))
    dft_ctx = tuple(t.astype(BF16) for t in dft_tables(CTX_LEN, (CTX_LEN * FNET_GW) ** -0.5))
    w_chan_dft = channel_dft_weight()

    xa = jnp.concatenate([x.reshape(N_LAT, D_MODEL), ctx.reshape(N_CTX, D_MODEL)], axis=0)
    for l in range(DEPTH):
        p = {
            'g_mix': g_mix[l], 'w_in': w_in,
            'lam_q1': lam_q1[l], 'lam_k1': lam_k1[l], 'lam_q2': lam_q2[l], 'lam_k2': lam_k2[l],
            'g_subln': g_subln[l], 'ssd_conv_w': ssd_conv_w[l], 'ssd_conv_b': ssd_conv_b[l],
            'ssd_dt_bias': ssd_dt_bias[l], 'ssd_a_log': ssd_a_log[l], 'ssd_d': ssd_d[l],
            'ssd_norm_g': ssd_norm_g[l], 'conf_conv_w': conf_conv_w[l], 'conf_conv_b': conf_conv_b[l],
            'conf_ln_g': conf_ln_g[l], 'conf_ln_b': conf_ln_b[l], 'w_branch': w_branch,
            'w_out': w_out, 'g_ffn': g_ffn[l], 'w_router': w_router[l], 'router_bias': router_bias[l],
            'w_exp_gu': w_exp_gu, 'w_exp_down': w_exp_down, 'w_sh_gu': w_sh_gu,
            'w_sh_down': w_sh_down,
            'dft_lat': dft_lat, 'dft_ctx': dft_ctx, 'w_chan_dft': w_chan_dft,
        }
        xa = trunk_layer(xa, mods[l], p, l, l < DEPTH - 1)
    zeros = jnp.zeros((BATCH + 1, D_MODEL), F32)
    y = rms_modulate(xa, g_final, zeros, zeros, m=N_LAT, out_dtype=F32)
    return y.reshape(BATCH, SEQ, D_MODEL)
```

```python
import functools
import math

import jax
import jax.numpy as jnp
from jax import lax
from jax.experimental import pallas as pl
from jax.experimental.pallas import tpu as pltpu

F32 = jnp.float32
BF16 = jnp.bfloat16

D_MODEL = 4096
BATCH = 2
SEQ = 4096
DEPTH = 2
CTX_LEN = 256
GRID_W = 64
N_BRANCH = 4
EPS = 1e-6
FNET_GROUPS = 4
FNET_GW = 256
FNET_W = FNET_GROUPS * FNET_GW
DIFF_HEADS = 8
DIFF_DK = 64
DIFF_DV = 128
DIFF_QW = DIFF_HEADS * 2 * DIFF_DK
DIFF_W = DIFF_HEADS * DIFF_DV
DIFF_SCALE = DIFF_DK ** -0.5
ROPE_BASE = 10000.0
ROPE_FREQS = DIFF_DK // 4
SSD_HEADS = 16
SSD_HD = 64
SSD_W = SSD_HEADS * SSD_HD
SSD_STATE = 128
SSD_GROUPS = 4
SSD_BC = SSD_GROUPS * SSD_STATE
SSD_CONV = 5
SSD_CHUNK = 128
CONF_W = 1024
CONF_K = 31
N_EXPERTS = 64
TOP_K = 6
N_EXPERT_GROUPS = 8
TOPK_GROUPS = 4
EXPERT_W = 256
SHARED_W = 1024
ROUTED_SCALE = 2.5

K_OFF = 0
V_OFF = K_OFF + DIFF_QW
SX_OFF = V_OFF + DIFF_W
SB_OFF = SX_OFF + SSD_W
SDT_OFF = SB_OFF + SSD_BC
SC_OFF = SDT_OFF + 2 * SSD_HEADS
Q_OFF = SC_OFF + SSD_BC
SZ_OFF = Q_OFF + DIFF_QW
FN_OFF = SZ_OFF + SSD_W
CF_OFF = FN_OFF + FNET_W
GATE_OFF = CF_OFF + 2 * CONF_W
N_IN = GATE_OFF + N_BRANCH * D_MODEL

A_K = 0
A_V = A_K + DIFF_QW
A_XBC = A_V + DIFF_W
A_W = A_XBC + SSD_W + 2 * SSD_BC
M_CF = 0
M_Q = M_CF + 2 * CONF_W
M_SZ = M_Q + DIFF_QW
M_FN = M_SZ + SSD_W
M_W = M_FN + FNET_W
W_STATE = 0
W_MIX = W_STATE + A_W
W_GATE = W_MIX + M_W
W_DT = W_GATE + N_BRANCH * D_MODEL
W_ALL = W_DT + 128

N_LAT = BATCH * SEQ
N_CTX = BATCH * CTX_LEN
N_ROWS = N_LAT + N_CTX
N_KEYS = SEQ + CTX_LEN

VMEM_LIMIT_BYTES = 56 * 1024 * 1024
LANE = 128
ROW_TILE = 512
COL_TILE = 512
ATTN_Q_TILE = 256
MOE_TILE = 256
MOE_ROWS_MAX = None


def _segment_of_row_tile(i, tm):
    return jnp.minimum(i * tm // SEQ, BATCH)


def _ws_kernel(*refs, n_a, a_of_w, n_extra, epilogue):
    n_w = len(a_of_w)
    a_refs = refs[:n_a]
    w_refs = refs[n_a:n_a + n_w]
    extra_refs = refs[n_a + n_w:n_a + n_w + n_extra]
    o_ref = refs[n_a + n_w + n_extra]
    scratch_refs = list(refs[n_a + n_w + n_extra + 1:])
    wb_refs = [w_ref if w_ref.dtype == BF16 else scratch_refs.pop(0) for w_ref in w_refs]

    @pl.when(pl.program_id(1) == 0)
    def _():
        for w_ref, wb_ref in zip(w_refs, wb_refs):
            if w_ref is not wb_ref:
                wb_ref[...] = w_ref[...].astype(BF16)

    a_vals = [a_ref[...] for a_ref in a_refs]
    dots = [jnp.dot(a_vals[ai], wb_ref[...], preferred_element_type=F32)
            for ai, wb_ref in zip(a_of_w, wb_refs)]
    o_ref[...] = epilogue(dots, [e[...] for e in extra_refs]).astype(o_ref.dtype)


def ws_matmul(a_list, w_list, a_of_w, epilogue, extras=(), *, m, n, out_dtype, name, tm=ROW_TILE, tn=COL_TILE):
    assert m % tm == 0 and n % tn == 0
    in_specs = []
    operands = []
    for a in a_list:
        if isinstance(a, tuple):
            a, width, cb = a
            in_specs.append(pl.BlockSpec((tm, width), lambda j, i, cb=cb: (i, cb)))
        else:
            in_specs.append(pl.BlockSpec((tm, a.shape[1]), lambda j, i: (i, 0)))
        operands.append(a)
    scratch = []
    for w, prefix, colfn in w_list:
        kdim = w.shape[-2]
        in_specs.append(pl.BlockSpec((None,) * len(prefix) + (kdim, tn),
                                     lambda j, i, prefix=prefix, colfn=colfn: prefix + (0, colfn(j))))
        operands.append(w)
        if w.dtype != BF16:
            scratch.append(pltpu.VMEM((kdim, tn), BF16))
    for ex in extras:
        if ex[0] == 'tile':
            _, arr, colfn = ex
            in_specs.append(pl.BlockSpec((tm, tn), lambda j, i, colfn=colfn: (i, colfn(j))))
        else:
            _, arr = ex
            in_specs.append(pl.BlockSpec((None, 1, tn), lambda j, i: (_segment_of_row_tile(i, tm), 0, j)))
        operands.append(arr)
    kern = functools.partial(_ws_kernel, n_a=len(a_list), a_of_w=tuple(a_of_w), n_extra=len(extras),
                             epilogue=epilogue)
    return pl.pallas_call(
        kern,
        name=name,
        grid=(n // tn, m // tm),
        in_specs=in_specs,
        out_specs=pl.BlockSpec((tm, tn), lambda j, i: (i, j)),
        out_shape=jax.ShapeDtypeStruct((m, n), out_dtype),
        scratch_shapes=scratch,
        compiler_params=pltpu.CompilerParams(
            dimension_semantics=("parallel", "arbitrary"), vmem_limit_bytes=VMEM_LIMIT_BYTES),
    )(*operands)


def _ident(dots, extras):
    return dots[0]


def project(h, w, col_start, n, *, m, name, prefix=(), out_dtype=BF16, tn=COL_TILE):
    assert col_start % tn == 0
    off = col_start // tn
    return ws_matmul([h], [(w, prefix, lambda j: j + off)], [0], _ident, m=m, n=n, out_dtype=out_dtype, tn=tn,
                     name=name)


def _ada_kernel(c_ref, w_ref, b_ref, o_ref):
    o_ref[...] = jnp.dot(c_ref[...], w_ref[...].astype(BF16), preferred_element_type=F32) + b_ref[...]


def ada_modulation(cc, w_ada, b_ada, tn=COL_TILE):
    nl, d, n = w_ada.shape
    return pl.pallas_call(
        _ada_kernel,
        name="ada_modulation",
        grid=(nl, n // tn),
        in_specs=[pl.BlockSpec((16, d), lambda l, j: (0, 0)),
                  pl.BlockSpec((None, d, tn), lambda l, j: (l, 0, j)),
                  pl.BlockSpec((None, 1, tn), lambda l, j: (l, 0, j))],
        out_specs=pl.BlockSpec((None, 16, tn), lambda l, j: (l, 0, j)),
        out_shape=jax.ShapeDtypeStruct((nl, 16, n), F32),
        compiler_params=pltpu.CompilerParams(
            dimension_semantics=("parallel", "parallel"), vmem_limit_bytes=VMEM_LIMIT_BYTES),
    )(cc, w_ada, b_ada.reshape(nl, 1, n))


def _rope(u, cos, sin_signed, first_half):
    partner = jnp.where(first_half, pltpu.roll(u, LANE - ROPE_FREQS, axis=1), pltpu.roll(u, ROPE_FREQS, axis=1))
    return u * cos + partner * sin_signed


def _attn_kernel(lam_ref, q_ref, k_ref, v_ref, cq_ref, sq_ref, ck_ref, sk_ref, g_ref, o_ref, k_s,
                 *, out_scale):
    lane = lax.broadcasted_iota(jnp.int32, (1, LANE), 1)
    first_half = (lane % (2 * ROPE_FREQS)) < ROPE_FREQS
    comp0 = lane < DIFF_DK

    @pl.when(pl.program_id(2) == 0)
    def _():
        k_s[...] = _rope(k_ref[...].astype(F32), ck_ref[...], sk_ref[...], first_half).astype(BF16)

    q = _rope(q_ref[...].astype(F32), cq_ref[...], sq_ref[...], first_half) * DIFF_SCALE
    k = k_s[...]
    lam = lam_ref[0]
    probs = []
    for c in range(2):
        qc = jnp.where(comp0 if c == 0 else jnp.logical_not(comp0), q, 0.0).astype(BF16)
        s = lax.dot_general(qc, k, (((1,), (1,)), ((), ())), preferred_element_type=F32)
        s = s - jnp.max(s, axis=-1, keepdims=True)
        p = jnp.exp(s)
        r = 1.0 / jnp.sum(p, axis=-1, keepdims=True)
        probs.append(p * (r if c == 0 else r * lam))
    a = (probs[0] - probs[1]).astype(BF16)
    o = jnp.dot(a, v_ref[...], preferred_element_type=F32)
    o = o * lax.rsqrt(jnp.mean(o * o, axis=-1, keepdims=True) + EPS) * (g_ref[...] * out_scale)
    o_ref[...] = o.astype(o_ref.dtype)


def diff_attention(q, kv_all, rope_q, rope_k, lam, g_subln, lam_init, *, q_col_block):
    bsz, lq = q.shape[:2]
    keys = kv_all.shape[1]
    tq = min(ATTN_Q_TILE, lq)
    cq, sq = rope_q
    ck, sk = rope_k
    const = dict(pipeline_mode=pl.Buffered(1))
    kern = functools.partial(_attn_kernel, out_scale=1.0 - lam_init)
    return pl.pallas_call(
        kern,
        name="diff_attention",
        grid=(bsz, DIFF_HEADS, lq // tq),
        in_specs=[pl.BlockSpec(memory_space=pltpu.SMEM),
                  pl.BlockSpec((None, tq, LANE), lambda b, h, i: (b, i, q_col_block + h)),
                  pl.BlockSpec((None, keys, LANE), lambda b, h, i: (b, 0, h)),
                  pl.BlockSpec((None, keys, LANE), lambda b, h, i: (b, 0, DIFF_HEADS + h)),
                  pl.BlockSpec((tq, LANE), lambda b, h, i: (i, 0)),
                  pl.BlockSpec((tq, LANE), lambda b, h, i: (i, 0)),
                  pl.BlockSpec((keys, LANE), lambda b, h, i: (0, 0), **const),
                  pl.BlockSpec((keys, LANE), lambda b, h, i: (0, 0), **const),
                  pl.BlockSpec((1, LANE), lambda b, h, i: (0, 0))],
        out_specs=pl.BlockSpec((None, tq, LANE), lambda b, h, i: (b, i, h)),
        out_shape=jax.ShapeDtypeStruct((bsz, lq, DIFF_W), BF16),
        scratch_shapes=[pltpu.VMEM((keys, LANE), BF16)],
        compiler_params=pltpu.CompilerParams(
            dimension_semantics=("parallel", "parallel", "arbitrary"), vmem_limit_bytes=VMEM_LIMIT_BYTES),
    )(lam.reshape(1).astype(F32), q, kv_all, kv_all, cq, sq, ck, sk, g_subln.reshape(1, LANE).astype(F32))


def rope_tables(s):
    pos = jnp.arange(s, dtype=jnp.int32)
    row = (pos // GRID_W).astype(F32)[:, None]
    col = (pos % GRID_W).astype(F32)[:, None]
    lane = jnp.arange(LANE, dtype=jnp.int32)
    d = lane % DIFF_DK
    inv = ROPE_BASE ** (-(d % ROPE_FREQS).astype(F32) / ROPE_FREQS)
    ang = jnp.where((d < DIFF_DK // 2)[None, :], row * inv[None, :], col * inv[None, :])
    sign = jnp.where((d % (2 * ROPE_FREQS)) < ROPE_FREQS, -1.0, 1.0)
    return jnp.cos(ang), jnp.sin(ang) * sign[None, :]


HALF_D = D_MODEL // 2
HI_MASK = 0xFFFF0000
ROW_BITS = 14
assert N_ROWS <= 1 << ROW_BITS


def _pack_halves(y):
    n = y.shape[1] // 2
    lo = lax.bitcast_convert_type(y[:, :n].astype(BF16).astype(F32), jnp.uint32)
    hi = lax.bitcast_convert_type(y[:, n:].astype(BF16).astype(F32), jnp.uint32)
    return (lo >> 16) | (hi & jnp.uint32(HI_MASK))


def _unpack_halves(w):
    lo = lax.bitcast_convert_type(w << 16, F32)
    hi = lax.bitcast_convert_type(w & jnp.uint32(HI_MASK), F32)
    return lo, hi


def _moe_kernel(te_ref, nu_ref, r0_ref, tv_ref, plan_ref, x_hbm, wgu_ref, wdn_ref, y_hbm,
                xbuf, ybuf, pad_s, wgu_s, wdn_s, gsem, ssem):
    t = pl.program_id(0)
    n_used = nu_ref[0]
    slot = t % 2
    active = t < n_used

    def gather_rows(tile, dst_slot):
        base = r0_ref[tile]

        def body(r, carry):
            src = plan_ref[base + r] & ((1 << ROW_BITS) - 1)
            pltpu.make_async_copy(x_hbm.at[pl.ds(src, 1)], xbuf.at[dst_slot, pl.ds(r, 1)],
                                  gsem.at[dst_slot]).start()
            return carry
        lax.fori_loop(0, MOE_TILE, body, 0, unroll=8)

    def scatter_rows(tile, src_slot):
        base = r0_ref[tile]
        n_valid = tv_ref[tile]

        def body(r, carry):
            dst = plan_ref[base + r] >> ROW_BITS
            pltpu.make_async_copy(ybuf.at[src_slot, pl.ds(r, 1)], y_hbm.at[pl.ds(dst, 1)],
                                  ssem.at[src_slot]).start()
            return carry

        def pad_body(r, carry):
            pltpu.make_async_copy(ybuf.at[src_slot, pl.ds(r, 1)], pad_s.at[src_slot, pl.ds(r, 1)],
                                  ssem.at[src_slot]).start()
            return carry

        @pl.when(n_valid == MOE_TILE)
        def _():
            lax.fori_loop(0, MOE_TILE, body, 0, unroll=8)

        @pl.when(n_valid < MOE_TILE)
        def _():
            lax.fori_loop(0, n_valid, body, 0)
            lax.fori_loop(n_valid, MOE_TILE, pad_body, 0)

    def wait_gather(s):
        pltpu.make_async_copy(x_hbm.at[pl.ds(0, MOE_TILE)], xbuf.at[s], gsem.at[s]).wait()

    def wait_scatter(s):
        pltpu.make_async_copy(ybuf.at[s], y_hbm.at[pl.ds(0, MOE_TILE)], ssem.at[s]).wait()

    @pl.when(t == 0)
    def _():
        gather_rows(0, 0)

    @pl.when(t + 1 < n_used)
    def _():
        gather_rows(t + 1, 1 - slot)

    @pl.when(jnp.logical_and(t >= 2, t - 2 < n_used))
    def _():
        wait_scatter(slot)

    prev = te_ref[jnp.maximum(t - 1, 0)]

    @pl.when(jnp.logical_or(t == 0, te_ref[t] != prev))
    def _():
        wgu_s[...] = wgu_ref[...].astype(BF16)
        wdn_s[...] = wdn_ref[...].astype(BF16)

    @pl.when(active)
    def _():
        wait_gather(slot)
        lo, hi = _unpack_halves(xbuf[slot])
        gu = (jnp.dot(lo.astype(BF16), wgu_s[:HALF_D], preferred_element_type=F32)
              + jnp.dot(hi.astype(BF16), wgu_s[HALF_D:], preferred_element_type=F32))
        gate = gu[:, :EXPERT_W]
        act = (gate * jax.nn.sigmoid(gate) * gu[:, EXPERT_W:]).astype(BF16)
        ybuf[slot] = _pack_halves(jnp.dot(act, wdn_s[...], preferred_element_type=F32))
        scatter_rows(t, slot)

    @pl.when(t == pl.num_programs(0) - 1)
    def _():
        @pl.when(jnp.logical_and(t >= 1, t - 1 < n_used))
        def _():
            wait_scatter(1 - slot)

        @pl.when(active)
        def _():
            wait_scatter(slot)


def moe_grouped(x_packed, tile_expert, n_used, tile_row0, tile_valid, plan, w_gu, w_down, layer):
    n_tok = x_packed.shape[0]
    n_tiles = tile_expert.shape[0]
    n_out = n_tok * TOP_K
    buf = pltpu.VMEM((2, MOE_TILE, HALF_D), jnp.uint32)
    wmap = lambda t, te, nu, r0, tv, plan: (layer, te[t], 0, 0)
    return pl.pallas_call(
        _moe_kernel,
        name="moe_experts",
        grid_spec=pltpu.PrefetchScalarGridSpec(
            num_scalar_prefetch=5,
            grid=(n_tiles,),
            in_specs=[pl.BlockSpec(memory_space=pl.ANY),
                      pl.BlockSpec((None, None, D_MODEL, 2 * EXPERT_W), wmap),
                      pl.BlockSpec((None, None, EXPERT_W, D_MODEL), wmap)],
            out_specs=pl.BlockSpec(memory_space=pl.ANY),
            scratch_shapes=[buf, buf, buf,
                            pltpu.VMEM((D_MODEL, 2 * EXPERT_W), BF16),
                            pltpu.VMEM((EXPERT_W, D_MODEL), BF16),
                            pltpu.SemaphoreType.DMA((2,)),
                            pltpu.SemaphoreType.DMA((2,))]),
        out_shape=jax.ShapeDtypeStruct((n_out, HALF_D), jnp.uint32),
        compiler_params=pltpu.CompilerParams(
            dimension_semantics=("arbitrary",), vmem_limit_bytes=VMEM_LIMIT_BYTES),
    )(tile_expert, n_used, tile_row0, tile_valid, plan, x_packed, w_gu, w_down)


def _combine_kernel(w_ref, *refs):
    y_refs, o_ref = refs[:TOP_K], refs[TOP_K]
    w = w_ref[...]
    acc_lo = acc_hi = None
    for k in range(TOP_K):
        lo, hi = _unpack_halves(y_refs[k][...])
        wk = w[:, k:k + 1]
        acc_lo = lo * wk if acc_lo is None else acc_lo + lo * wk
        acc_hi = hi * wk if acc_hi is None else acc_hi + hi * wk
    o_ref[:, :HALF_D] = acc_lo.astype(o_ref.dtype)
    o_ref[:, HALF_D:] = acc_hi.astype(o_ref.dtype)


def moe_combine(y, w, t, tm=256):
    blocks = t // tm
    return pl.pallas_call(
        _combine_kernel,
        name="moe_combine",
        grid=(blocks,),
        in_specs=[pl.BlockSpec((tm, TOP_K), lambda i: (i, 0))]
                 + [pl.BlockSpec((tm, HALF_D), lambda i, k=k: (k * blocks + i, 0)) for k in range(TOP_K)],
        out_specs=pl.BlockSpec((tm, D_MODEL), lambda i: (i, 0)),
        out_shape=jax.ShapeDtypeStruct((t, D_MODEL), BF16),
        compiler_params=pltpu.CompilerParams(
            dimension_semantics=("parallel",), vmem_limit_bytes=VMEM_LIMIT_BYTES),
    )(w, *([y] * TOP_K))


def route(h, w_router, router_bias):
    logits = jnp.dot(h.astype(F32), w_router, precision=lax.Precision.HIGHEST)
    scores = jax.nn.sigmoid(logits)
    sel = scores + router_bias
    grp = sel.reshape(-1, N_EXPERT_GROUPS, N_EXPERTS // N_EXPERT_GROUPS)
    grp_score = jnp.sum(lax.top_k(grp, 2)[0], axis=-1)
    _, gidx = lax.top_k(grp_score, TOPK_GROUPS)
    gmask = jnp.any(gidx[..., :, None] == jnp.arange(N_EXPERT_GROUPS), axis=-2)
    emask = jnp.repeat(gmask, N_EXPERTS // N_EXPERT_GROUPS, axis=-1)
    _, eidx = lax.top_k(jnp.where(emask, sel, -jnp.inf), TOP_K)
    w = jnp.take_along_axis(scores, eidx, axis=-1)
    w = w / jnp.sum(w, axis=-1, keepdims=True) * ROUTED_SCALE
    return eidx, w


def dispatch_plan(eidx):
    t = eidx.shape[0]
    n_assign = eidx.size
    n_tiles = n_assign // MOE_TILE + N_EXPERTS
    eflat = eidx.reshape(n_assign).astype(jnp.int32)
    _, order = lax.sort_key_val(eflat, jnp.arange(n_assign, dtype=jnp.int32))
    counts = jnp.sum((eflat[:, None] == jnp.arange(N_EXPERTS, dtype=jnp.int32)[None, :]).astype(jnp.int32), axis=0)
    tiles_per = (counts + MOE_TILE - 1) // MOE_TILE
    tile_end = jnp.cumsum(tiles_per)
    tile_start = tile_end - tiles_per
    grp_start = jnp.cumsum(counts) - counts
    n_used = tile_end[-1:]
    tile_ids = jnp.arange(n_tiles, dtype=jnp.int32)
    tile_expert = jnp.sum((tile_ids[:, None] >= tile_end[None, :]).astype(jnp.int32), axis=1)
    tile_expert = jnp.minimum(tile_expert, tile_expert[n_used[0] - 1])
    in_expert = (tile_ids - tile_start[tile_expert]) * MOE_TILE
    used = tile_ids < n_used[0]
    tile_row0 = jnp.where(used, grp_start[tile_expert] + in_expert, 0)
    tile_valid = jnp.where(used, jnp.clip(counts[tile_expert] - in_expert, 0, MOE_TILE), 0)
    token = order // TOP_K
    plan = (((order % TOP_K) * t + token) << ROW_BITS) | token
    filler = jnp.arange(MOE_TILE, dtype=jnp.int32) % t
    return jnp.concatenate([plan, filler]), tile_expert, tile_row0, tile_valid, n_used


def routed_experts(h, h_packed, p, layer):
    t = h.shape[0]
    eidx, w = route(h, p['w_router'], p['router_bias'])
    plan, tile_expert, tile_row0, tile_valid, n_used = dispatch_plan(eidx)
    y = moe_grouped(h_packed, tile_expert, n_used, tile_row0, tile_valid, plan, p['w_exp_gu'], p['w_exp_down'],
                    layer)
    return moe_combine(y, w, t)


def dft_tables(l, scale):
    f = 1
    while f * f < l:
        f *= 2
    assert l % f == 0
    k = jnp.arange(l, dtype=jnp.int32)[None, :]
    j1 = jnp.arange(l // f, dtype=jnp.int32)[:, None]
    j2 = jnp.arange(f, dtype=jnp.int32)[:, None]
    alpha = (2.0 * math.pi / l) * ((j1 * f * k) % l).astype(F32)
    beta = (2.0 * math.pi / l) * ((j2 * k) % l).astype(F32)
    ca, sa, cb, sb = jnp.cos(alpha)[:, None], jnp.sin(alpha)[:, None], jnp.cos(beta)[None], jnp.sin(beta)[None]
    cos = (ca * cb - sa * sb).reshape(l, l) * scale
    sin = (sa * cb + ca * sb).reshape(l, l) * scale
    return cos, sin


def channel_dft_weight():
    cos, sin = dft_tables(FNET_GW, 1.0)
    eye = jnp.eye(FNET_GROUPS, dtype=F32)
    return jnp.concatenate([jnp.kron(eye, cos), jnp.kron(eye, sin)], axis=1)


def _dft_seq_kernel(c_ref, s_ref, p_ref, q_ref, o_ref):
    o_ref[...] = (jnp.dot(c_ref[...], p_ref[...], preferred_element_type=F32)
                  - jnp.dot(s_ref[...], q_ref[...], preferred_element_type=F32)).astype(o_ref.dtype)


def fourier_mix(pq, tables, *, row_start, l, n_seq, tm=ROW_TILE, tn=COL_TILE):
    cos, sin = tables
    tm = min(tm, l)
    seq0 = row_start // l
    nj = FNET_W // tn
    return pl.pallas_call(
        _dft_seq_kernel,
        name="fourier_seq",
        grid=(n_seq, nj, l // tm),
        in_specs=[pl.BlockSpec((tm, l), lambda b, j, i: (i, 0)),
                  pl.BlockSpec((tm, l), lambda b, j, i: (i, 0)),
                  pl.BlockSpec((l, tn), lambda b, j, i: (seq0 + b, j)),
                  pl.BlockSpec((l, tn), lambda b, j, i: (seq0 + b, nj + j))],
        out_specs=pl.BlockSpec((tm, tn), lambda b, j, i: (b * (l // tm) + i, j)),
        out_shape=jax.ShapeDtypeStruct((n_seq * l, FNET_W), BF16),
        compiler_params=pltpu.CompilerParams(
            dimension_semantics=("parallel", "parallel", "arbitrary"), vmem_limit_bytes=VMEM_LIMIT_BYTES),
    )(cos, sin, pq, pq)


CONF_HALO = 16


CONF_ROWS = 64


def _conformer_kernel(prev_ref, a_ref, g_ref, next_ref, pg_ref, ng_ref, w_ref, b_ref, lg_ref, lb_ref,
                      o_ref, v_s, c_s, *, tl):
    i = pl.program_id(1)
    last = pl.num_programs(1) - 1

    def glu(a, g):
        a = a.astype(F32)
        return a * jax.nn.sigmoid(g.astype(F32))

    v_s[0:CONF_HALO] = jnp.where(i > 0, glu(prev_ref[...], pg_ref[...]), 0.0)
    v_s[CONF_HALO:CONF_HALO + tl] = glu(a_ref[...], g_ref[...])
    v_s[CONF_HALO + tl:] = jnp.where(i < last, glu(next_ref[...], ng_ref[...]), 0.0)
    for c in range(CONF_W // LANE):
        lanes = slice(c * LANE, (c + 1) * LANE)
        w = w_ref[:, lanes]
        for rb in range(tl // CONF_ROWS):
            acc = jnp.zeros((CONF_ROWS, LANE), F32) + b_ref[:, lanes]
            for k in range(CONF_K):
                off = rb * CONF_ROWS + CONF_HALO - CONF_K // 2 + k
                acc = acc + v_s[off:off + CONF_ROWS, lanes] * w[k:k + 1]
            c_s[rb * CONF_ROWS:(rb + 1) * CONF_ROWS, lanes] = acc
    acc = c_s[...]
    mu = jnp.mean(acc, axis=-1, keepdims=True)
    cen = acc - mu
    var = jnp.mean(cen * cen, axis=-1, keepdims=True)
    y = cen * lax.rsqrt(var + EPS) * lg_ref[...] + lb_ref[...]
    o_ref[...] = (y * jax.nn.sigmoid(y)).astype(o_ref.dtype)


def conformer_conv(proj, p, *, col_block, row_start, l, n_seq, tl=256):
    tl = min(tl, l)
    nt = l // tl
    hb = tl // CONF_HALO
    first = row_start // CONF_HALO

    def main(c):
        return pl.BlockSpec((tl, CONF_W), lambda b, i: (row_start // tl + b * nt + i, col_block + c))

    def prev(c):
        return pl.BlockSpec((CONF_HALO, CONF_W),
                            lambda b, i: (first + jnp.maximum((b * nt + i) * hb - 1, 0), col_block + c))

    def nxt(c):
        return pl.BlockSpec((CONF_HALO, CONF_W),
                            lambda b, i: (first + jnp.minimum((b * nt + i + 1) * hb, n_seq * nt * hb - 1),
                                          col_block + c))

    vec = pl.BlockSpec((1, CONF_W), lambda b, i: (0, 0))
    kern = functools.partial(_conformer_kernel, tl=tl)
    return pl.pallas_call(
        kern,
        name="conformer_conv",
        grid=(n_seq, nt),
        in_specs=[prev(0), main(0), main(1), nxt(0), prev(1), nxt(1),
                  pl.BlockSpec((CONF_K, CONF_W), lambda b, i: (0, 0)), vec, vec, vec],
        out_specs=pl.BlockSpec((tl, CONF_W), lambda b, i: (b * nt + i, 0)),
        out_shape=jax.ShapeDtypeStruct((n_seq * l, CONF_W), BF16),
        scratch_shapes=[pltpu.VMEM((tl + 2 * CONF_HALO, CONF_W), F32), pltpu.VMEM((tl, CONF_W), F32)],
        compiler_params=pltpu.CompilerParams(
            dimension_semantics=("parallel", "parallel"), vmem_limit_bytes=VMEM_LIMIT_BYTES),
    )(proj, proj, proj, proj, proj, proj, p['conf_conv_w'], p['conf_conv_b'].reshape(1, CONF_W),
      p['conf_ln_g'].reshape(1, CONF_W), p['conf_ln_b'].reshape(1, CONF_W))


SSD_XBC = SSD_W + 2 * SSD_BC
SSD_HALO = 16
SSD_CONV_TILE = 256
HEADS_PER_GROUP = SSD_HEADS // SSD_GROUPS
GROUP_W = HEADS_PER_GROUP * SSD_HD


def _ssd_conv_kernel(prev_ref, x_ref, next_ref, w_ref, b_ref, o_ref, v_s):
    tl = SSD_CONV_TILE
    i = pl.program_id(0)
    lat_tiles = N_LAT // tl
    per_seq = jnp.where(i < lat_tiles, SEQ // tl, CTX_LEN // tl)
    pos = jnp.where(i < lat_tiles, i % (SEQ // tl), (i - lat_tiles) % (CTX_LEN // tl))
    v_s[0:SSD_HALO] = jnp.where(pos > 0, prev_ref[...].astype(F32), 0.0)
    v_s[SSD_HALO:SSD_HALO + tl] = x_ref[...].astype(F32)
    v_s[SSD_HALO + tl:] = jnp.where(pos < per_seq - 1, next_ref[...].astype(F32), 0.0)
    for c in range(SSD_XBC // LANE):
        lanes = slice(c * LANE, (c + 1) * LANE)
        w = w_ref[:, lanes]
        for rb in range(tl // CONF_ROWS):
            acc = jnp.zeros((CONF_ROWS, LANE), F32) + b_ref[:, lanes]
            for k in range(SSD_CONV):
                off = rb * CONF_ROWS + SSD_HALO - SSD_CONV // 2 + k
                acc = acc + v_s[off:off + CONF_ROWS, lanes] * w[k:k + 1]
            o_ref[rb * CONF_ROWS:(rb + 1) * CONF_ROWS, lanes] = (acc * jax.nn.sigmoid(acc)).astype(o_ref.dtype)


def ssd_conv(proj, conv_w, conv_b, *, col_block):
    tl = SSD_CONV_TILE
    hb = tl // SSD_HALO
    n_tiles = N_ROWS // tl
    return pl.pallas_call(
        _ssd_conv_kernel,
        name="ssd_conv",
        grid=(n_tiles,),
        in_specs=[pl.BlockSpec((SSD_HALO, SSD_XBC), lambda i: (jnp.maximum(i * hb - 1, 0), col_block)),
                  pl.BlockSpec((tl, SSD_XBC), lambda i: (i, col_block)),
                  pl.BlockSpec((SSD_HALO, SSD_XBC),
                               lambda i: (jnp.minimum((i + 1) * hb, n_tiles * hb - 1), col_block)),
                  pl.BlockSpec((SSD_CONV, SSD_XBC), lambda i: (0, 0)),
                  pl.BlockSpec((1, SSD_XBC), lambda i: (0, 0))],
        out_specs=pl.BlockSpec((tl, SSD_XBC), lambda i: (i, 0)),
        out_shape=jax.ShapeDtypeStruct((N_ROWS, SSD_XBC), BF16),
        scratch_shapes=[pltpu.VMEM((tl + 2 * SSD_HALO, SSD_XBC), F32)],
        compiler_params=pltpu.CompilerParams(
            dimension_semantics=("parallel",), vmem_limit_bytes=VMEM_LIMIT_BYTES),
    )(proj, proj, proj, conv_w, conv_b.reshape(1, SSD_XBC))


def _split3(v):
    hi = v.astype(BF16)
    r1 = v - hi.astype(F32)
    mid = r1.astype(BF16)
    lo = (r1 - mid.astype(F32)).astype(BF16)
    return hi, mid, lo


def _ssd_scan_kernel(x_ref, b_ref, c_ref, dt_ref, a_ref, bias_ref, y_ref, st, *, reverse):
    @pl.when(pl.program_id(1) == 0)
    def _():
        st[...] = jnp.zeros_like(st)

    t = SSD_CHUNK
    col0 = SSD_HEADS if reverse else 0
    dt_in = dt_ref[:, col0:col0 + SSD_HEADS] + bias_ref[...]
    dt = jnp.maximum(dt_in, 0.0) + jnp.log(1.0 + jnp.exp(-jnp.abs(dt_in)))
    ad = dt * a_ref[...]
    row = lax.broadcasted_iota(jnp.int32, (t, t), 0)
    col = lax.broadcasted_iota(jnp.int32, (t, t), 1)
    keep = (col >= row) if reverse else (col <= row)
    tri = jnp.where(keep, 1.0, 0.0).astype(BF16)
    cs = sum(jnp.dot(tri, piece, preferred_element_type=F32) for piece in _split3(ad))
    cs_t = cs.T
    total = cs[0:1] if reverse else cs[t - 1:t]
    w_end = jnp.exp(total - cs)
    e_cs = jnp.exp(cs)
    e_tot = jnp.exp(total)
    head_of_lane = lax.broadcasted_iota(jnp.int32, (1, GROUP_W), 1) // SSD_HD

    def per_lane(v, g):
        out = v[:, g * HEADS_PER_GROUP:g * HEADS_PER_GROUP + 1]
        for hl in range(1, HEADS_PER_GROUP):
            h = g * HEADS_PER_GROUP + hl
            out = jnp.where(head_of_lane >= hl, v[:, h:h + 1], out)
        return out

    for g in range(SSD_GROUPS):
        bg = b_ref[:, g * SSD_STATE:(g + 1) * SSD_STATE]
        cg = c_ref[:, g * SSD_STATE:(g + 1) * SSD_STATE]
        cols = slice(g * GROUP_W, (g + 1) * GROUP_W)
        gmat = lax.dot_general(cg, bg, (((1,), (1,)), ((), ())), preferred_element_type=F32)
        st_g = st[g]
        xd = x_ref[:, cols].astype(F32) * per_lane(dt, g)
        xd_b = xd.astype(BF16)
        y_g = jnp.dot(cg, st_g.astype(BF16), preferred_element_type=F32) * per_lane(e_cs, g)
        y_diag = None
        for hl in range(HEADS_PER_GROUP):
            h = g * HEADS_PER_GROUP + hl
            lmat = jnp.exp(jnp.where(keep, cs[:, h:h + 1] - cs_t[h:h + 1, :], -jnp.inf))
            r = jnp.dot((gmat * lmat).astype(BF16), xd_b, preferred_element_type=F32)
            y_diag = r if y_diag is None else jnp.where(head_of_lane >= hl, r, y_diag)
        y_ref[:, cols] = (y_g + y_diag).astype(y_ref.dtype)
        upd = lax.dot_general(bg, (xd * per_lane(w_end, g)).astype(BF16), (((0,), (0,)), ((), ())),
                              preferred_element_type=F32)
        st[g] = st_g * per_lane(e_tot, g) + upd


def ssd_scan(u, dt_raw, a_log, dt_bias, *, reverse):
    d = 1 if reverse else 0
    ctx_chunks = CTX_LEN // SSD_CHUNK
    lat_chunks = SEQ // SSD_CHUNK
    n_chunks = ctx_chunks + lat_chunks

    def rows(b, s):
        ctx_i = (ctx_chunks - 1 - s) if reverse else s
        lat_i = (n_chunks - 1 - s) if reverse else (s - ctx_chunks)
        return jnp.where(s < ctx_chunks, N_LAT // SSD_CHUNK + b * ctx_chunks + ctx_i, b * lat_chunks + lat_i)

    xw = SSD_W // SSD_BC
    kern = functools.partial(_ssd_scan_kernel, reverse=reverse)
    return pl.pallas_call(
        kern,
        name="ssd_scan",
        grid=(BATCH, n_chunks),
        in_specs=[pl.BlockSpec((SSD_CHUNK, SSD_W), lambda b, s: (rows(b, s), 0)),
                  pl.BlockSpec((SSD_CHUNK, SSD_BC), lambda b, s: (rows(b, s), xw)),
                  pl.BlockSpec((SSD_CHUNK, SSD_BC), lambda b, s: (rows(b, s), xw + 1)),
                  pl.BlockSpec((SSD_CHUNK, LANE), lambda b, s: (rows(b, s), 0)),
                  pl.BlockSpec((1, SSD_HEADS), lambda b, s: (0, 0)),
                  pl.BlockSpec((1, SSD_HEADS), lambda b, s: (0, 0))],
        out_specs=pl.BlockSpec((SSD_CHUNK, SSD_W), lambda b, s: (rows(b, s), 0)),
        out_shape=jax.ShapeDtypeStruct((N_ROWS, SSD_W), BF16),
        scratch_shapes=[pltpu.VMEM((SSD_GROUPS, SSD_STATE, GROUP_W), F32)],
        compiler_params=pltpu.CompilerParams(
            dimension_semantics=("parallel", "arbitrary"), vmem_limit_bytes=VMEM_LIMIT_BYTES),
    )(u, u, u, dt_raw, -jnp.exp(a_log[d]).reshape(1, SSD_HEADS), dt_bias[d].reshape(1, SSD_HEADS))


def _ssd_finish_kernel(yf_ref, yb_ref, x_ref, z_ref, d_ref, g_ref, o_ref):
    z = z_ref[...].astype(F32)
    y = yf_ref[...].astype(F32) + yb_ref[...].astype(F32) + d_ref[...] * x_ref[...].astype(F32)
    y = y * (z * jax.nn.sigmoid(z))
    gw = SSD_W // SSD_GROUPS
    for g in range(SSD_GROUPS):
        cols = slice(g * gw, (g + 1) * gw)
        yg = y[:, cols]
        yg = yg * lax.rsqrt(jnp.mean(yg * yg, axis=-1, keepdims=True) + EPS)
        o_ref[:, cols] = (yg * g_ref[:, cols]).astype(o_ref.dtype)


def ssd_finish(y_f, y_b, u, proj_mix, ssd_d, norm_g, *, m, z_col_block, tm=ROW_TILE):
    row = pl.BlockSpec((tm, SSD_W), lambda i: (i, 0))
    vec = pl.BlockSpec((1, SSD_W), lambda i: (0, 0))
    return pl.pallas_call(
        _ssd_finish_kernel,
        name="ssd_finish",
        grid=(m // tm,),
        in_specs=[row, row, row, pl.BlockSpec((tm, SSD_W), lambda i: (i, z_col_block)), vec, vec],
        out_specs=row,
        out_shape=jax.ShapeDtypeStruct((m, SSD_W), BF16),
        compiler_params=pltpu.CompilerParams(
            dimension_semantics=("parallel",), vmem_limit_bytes=VMEM_LIMIT_BYTES),
    )(y_f, y_b, u, proj_mix, jnp.repeat(ssd_d, SSD_HD).reshape(1, SSD_W), norm_g.reshape(1, SSD_W))


def _lat(a):
    return a[:N_LAT].reshape(BATCH, SEQ, a.shape[-1])


def _ctx(a):
    return a[N_LAT:N_ROWS].reshape(BATCH, CTX_LEN, a.shape[-1])


def _rows(lat, ctx=None):
    lat = lat.reshape(N_LAT, lat.shape[-1])
    if ctx is None:
        return lat
    return jnp.concatenate([lat, ctx.reshape(N_CTX, ctx.shape[-1])], axis=0)


def _merge_epilogue(dots, extras):
    acc = None
    for d, g in zip(dots, extras):
        term = jax.nn.sigmoid(g.astype(F32)) * d
        acc = term if acc is None else acc + term
    return acc


def _residual_epilogue(dots, extras):
    x, g = extras
    return x + g * dots[0]


def _ffn_out_epilogue(dots, extras):
    x, routed, g = extras
    return x + g * (dots[0] + routed)


def _glu_epilogue(dots, extras):
    gate, up = dots
    return gate * jax.nn.sigmoid(gate) * up


def _norm_kernel(x_ref, g_ref, sc_ref, sh_ref, o_ref, *packed_ref):
    x = x_ref[...]
    y = x * lax.rsqrt(jnp.mean(x * x, axis=-1, keepdims=True) + EPS) * g_ref[...]
    y = y * (1.0 + sc_ref[...]) + sh_ref[...]
    o_ref[...] = y.astype(o_ref.dtype)
    if packed_ref:
        packed_ref[0][...] = _pack_halves(y)


def rms_modulate(x, g, scale, shift, *, m, out_dtype=BF16, packed=False, tm=256):
    d = x.shape[1]
    seg_spec = pl.BlockSpec((None, 1, d), lambda i: (_segment_of_row_tile(i, tm), 0, 0))
    out_specs = [pl.BlockSpec((tm, d), lambda i: (i, 0))]
    out_shape = [jax.ShapeDtypeStruct((m, d), out_dtype)]
    if packed:
        out_specs.append(pl.BlockSpec((tm, d // 2), lambda i: (i, 0)))
        out_shape.append(jax.ShapeDtypeStruct((m, d // 2), jnp.uint32))
    outs = pl.pallas_call(
        _norm_kernel,
        name="rms_modulate",
        grid=(m // tm,),
        in_specs=[pl.BlockSpec((tm, d), lambda i: (i, 0)),
                  pl.BlockSpec((1, d), lambda i: (0, 0)),
                  seg_spec, seg_spec],
        out_specs=out_specs,
        out_shape=out_shape,
        compiler_params=pltpu.CompilerParams(
            dimension_semantics=("parallel",), vmem_limit_bytes=VMEM_LIMIT_BYTES),
    )(x, g.reshape(1, d), scale.reshape(-1, 1, d), shift.reshape(-1, 1, d))
    return outs if packed else outs[0]


def token_mixers(h, p, layer_idx, ctx_out, m_full):
    w_all = p['w_all']
    lp = (layer_idx,)
    proj_a = project(h, w_all, W_STATE, A_W, m=N_ROWS, prefix=lp, name="proj_state")
    proj_dt = project(h, w_all, W_DT, LANE, m=N_ROWS, prefix=lp, out_dtype=F32, tn=LANE, name="proj_dt")
    proj_b = project(h, w_all, W_MIX, M_W, m=m_full, prefix=lp, name="proj_mix")
    proj_g = project(h, w_all, W_GATE, N_BRANCH * D_MODEL, m=m_full, prefix=lp, name="proj_gate")

    lam_init = 0.8 - 0.6 * math.exp(-0.3 * layer_idx)
    lam = (jnp.exp(jnp.sum(p['lam_q1'] * p['lam_k1'])) - jnp.exp(jnp.sum(p['lam_q2'] * p['lam_k2'])) + lam_init)
    cos_l, sin_l = rope_tables(SEQ)
    cos_c, sin_c = jnp.ones((CTX_LEN, LANE), F32), jnp.zeros((CTX_LEN, LANE), F32)
    kv_lat = _lat(proj_a[:, :A_XBC])
    kv_ctx = _ctx(proj_a[:, :A_XBC])
    kv_all = jnp.concatenate([kv_lat, kv_ctx], axis=1)
    rope_k = (jnp.concatenate([cos_l, cos_c]), jnp.concatenate([sin_l, sin_c]))
    y_diff = diff_attention(_lat(proj_b), kv_all, (cos_l, sin_l), rope_k, lam, p['g_subln'], lam_init,
                            q_col_block=M_Q // LANE)
    if ctx_out:
        yc_diff = diff_attention(_ctx(proj_b), kv_ctx, (cos_c, sin_c), (cos_c, sin_c), lam, p['g_subln'],
                                 lam_init, q_col_block=M_Q // LANE)

    u = ssd_conv(proj_a, p['ssd_conv_w'], p['ssd_conv_b'], col_block=A_XBC // SSD_XBC)
    y_f = ssd_scan(u, proj_dt, p['ssd_a_log'], p['ssd_dt_bias'], reverse=False)
    y_b = ssd_scan(u, proj_dt, p['ssd_a_log'], p['ssd_dt_bias'], reverse=True)
    y_ssd = ssd_finish(y_f, y_b, u, proj_b, p['ssd_d'], p['ssd_norm_g'], m=m_full, z_col_block=M_SZ // SSD_W)

    pq = ws_matmul([(proj_b, FNET_W, M_FN // FNET_W)], [(p['w_chan_dft'], (), lambda j: j)], [0], _ident,
                   m=m_full, n=2 * FNET_W, out_dtype=BF16, name="fourier_chan")
    y_fn = fourier_mix(pq, p['dft_lat'], row_start=0, l=SEQ, n_seq=BATCH)
    y_cf = conformer_conv(proj_b, p, col_block=M_CF // CONF_W, row_start=0, l=SEQ, n_seq=BATCH)

    if ctx_out:
        yc_fn = fourier_mix(pq, p['dft_ctx'], row_start=N_LAT, l=CTX_LEN, n_seq=BATCH)
        yc_cf = conformer_conv(proj_b, p, col_block=M_CF // CONF_W, row_start=N_LAT, l=CTX_LEN, n_seq=BATCH)
        branches = [jnp.concatenate([y_fn, yc_fn]), _rows(y_diff, yc_diff), y_ssd, jnp.concatenate([y_cf, yc_cf])]
    else:
        branches = [y_fn, _rows(y_diff), y_ssd, y_cf]
    return branches, proj_g


def trunk_layer(x, mods, p, layer_idx, ctx_out):
    sh1, sc1, g1, sh2, sc2, g2 = jnp.split(mods, 6, axis=-1)
    m_full = N_ROWS if ctx_out else N_LAT
    lp = (layer_idx,)
    h = rms_modulate(x, p['g_mix'], sc1, sh1, m=N_ROWS)
    branches, proj_g = token_mixers(h, p, layer_idx, ctx_out, m_full)

    gate_blocks = D_MODEL // COL_TILE
    acc = ws_matmul(
        branches, [(p['w_branch'], (layer_idx, i), lambda j: j) for i in range(N_BRANCH)], list(range(N_BRANCH)),
        _merge_epilogue,
        extras=[('tile', proj_g, lambda j, i=i: i * gate_blocks + j) for i in range(N_BRANCH)],
        m=m_full, n=D_MODEL, out_dtype=BF16, name="merge_branches")
    x = ws_matmul([acc], [(p['w_out'], lp, lambda j: j)], [0], _residual_epilogue,
                  extras=[('tile', x, lambda j: j), ('seg', g1.reshape(3, 1, D_MODEL))],
                  m=m_full, n=D_MODEL, out_dtype=F32, name="mix_out")

    h2, h2_packed = rms_modulate(x, p['g_ffn'], sc2, sh2, m=m_full, packed=True)
    routed = routed_experts(h2, h2_packed, p, layer_idx)
    up_blocks = SHARED_W // COL_TILE
    act = ws_matmul([h2], [(p['w_sh_gu'], lp, lambda j: j), (p['w_sh_gu'], lp, lambda j: j + up_blocks)], [0, 0],
                    _glu_epilogue, m=m_full, n=SHARED_W, out_dtype=BF16, name="shared_glu")
    x = ws_matmul([act], [(p['w_sh_down'], lp, lambda j: j)], [0], _ffn_out_epilogue,
                  extras=[('tile', x, lambda j: j), ('tile', routed, lambda j: j),
                          ('seg', g2.reshape(3, 1, D_MODEL))],
                  m=m_full, n=D_MODEL, out_dtype=F32, name="ffn_out")
    return x


def kernel(x, c, ctx, c_ctx, w_ada, b_ada, g_mix, w_in, lam_q1, lam_k1, lam_q2, lam_k2, g_subln, ssd_conv_w, ssd_conv_b, ssd_dt_bias, ssd_a_log, ssd_d, ssd_norm_g, conf_conv_w, conf_conv_b, conf_ln_g, conf_ln_b, w_branch, w_out, g_ffn, w_router, router_bias, w_exp_gu, w_exp_down, w_sh_gu, w_sh_down, g_final):
    cc = jnp.concatenate([jax.nn.silu(c), jax.nn.silu(c_ctx)[None]], axis=0)
    cc = jnp.pad(cc, ((0, 16 - cc.shape[0]), (0, 0))).astype(BF16)
    mods = ada_modulation(cc, w_ada, b_ada)[:, :BATCH + 1]

    dft_lat = tuple(t.astype(BF16) for t in dft_tables(SEQ, (SEQ * FNET_GW) ** -0.5))
    dft_ctx = tuple(t.astype(BF16) for t in dft_tables(CTX_LEN, (CTX_LEN * FNET_GW) ** -0.5))
    w_chan_dft = channel_dft_weight()
    w_all = jnp.concatenate(
        [w_in[:, :, K_OFF:SDT_OFF], w_in[:, :, SC_OFF:Q_OFF],
         w_in[:, :, CF_OFF:GATE_OFF], w_in[:, :, Q_OFF:CF_OFF],
         w_in[:, :, GATE_OFF:],
         jnp.pad(w_in[:, :, SDT_OFF:SC_OFF], ((0, 0), (0, 0), (0, W_ALL - W_DT - 2 * SSD_HEADS)))],
        axis=2).astype(BF16)

    xa = jnp.concatenate([x.reshape(N_LAT, D_MODEL), ctx.reshape(N_CTX, D_MODEL)], axis=0)
    for l in range(DEPTH):
        p = {
            'g_mix': g_mix[l], 'w_all': w_all,
            'lam_q1': lam_q1[l], 'lam_k1': lam_k1[l], 'lam_q2': lam_q2[l], 'lam_k2': lam_k2[l],
            'g_subln': g_subln[l], 'ssd_conv_w': ssd_conv_w[l], 'ssd_conv_b': ssd_conv_b[l],
            'ssd_dt_bias': ssd_dt_bias[l], 'ssd_a_log': ssd_a_log[l], 'ssd_d': ssd_d[l],
            'ssd_norm_g': ssd_norm_g[l], 'conf_conv_w': conf_conv_w[l], 'conf_conv_b': conf_conv_b[l],
            'conf_ln_g': conf_ln_g[l], 'conf_ln_b': conf_ln_b[l], 'w_branch': w_branch,
            'w_out': w_out, 'g_ffn': g_ffn[l], 'w_router': w_router[l], 'router_bias': router_bias[l],
            'w_exp_gu': w_exp_gu, 'w_exp_down': w_exp_down, 'w_sh_gu': w_sh_gu,
            'w_sh_down': w_sh_down,
            'dft_lat': dft_lat, 'dft_ctx': dft_ctx, 'w_chan_dft': w_chan_dft,
        }
        xa = trunk_layer(xa, mods[l], p, l, l < DEPTH - 1)
    zeros = jnp.zeros((BATCH + 1, D_MODEL), F32)
    y = rms_modulate(xa, g_final, zeros, zeros, m=N_LAT, out_dtype=F32)
    return y.reshape(BATCH, SEQ, D_MODEL)
```

```python
import functools
import math

import jax
import jax.numpy as jnp
from jax import lax
from jax.experimental import pallas as pl
from jax.experimental.pallas import tpu as pltpu

F32 = jnp.float32
BF16 = jnp.bfloat16

D_MODEL = 4096
BATCH = 2
SEQ = 4096
DEPTH = 2
CTX_LEN = 256
GRID_W = 64
N_BRANCH = 4
EPS = 1e-6
FNET_GROUPS = 4
FNET_GW = 256
FNET_W = FNET_GROUPS * FNET_GW
DIFF_HEADS = 8
DIFF_DK = 64
DIFF_DV = 128
DIFF_QW = DIFF_HEADS * 2 * DIFF_DK
DIFF_W = DIFF_HEADS * DIFF_DV
DIFF_SCALE = DIFF_DK ** -0.5
ROPE_BASE = 10000.0
ROPE_FREQS = DIFF_DK // 4
SSD_HEADS = 16
SSD_HD = 64
SSD_W = SSD_HEADS * SSD_HD
SSD_STATE = 128
SSD_GROUPS = 4
SSD_BC = SSD_GROUPS * SSD_STATE
SSD_CONV = 5
SSD_CHUNK = 128
CONF_W = 1024
CONF_K = 31
N_EXPERTS = 64
TOP_K = 6
N_EXPERT_GROUPS = 8
TOPK_GROUPS = 4
EXPERT_W = 256
SHARED_W = 1024
ROUTED_SCALE = 2.5

K_OFF = 0
V_OFF = K_OFF + DIFF_QW
SX_OFF = V_OFF + DIFF_W
SB_OFF = SX_OFF + SSD_W
SDT_OFF = SB_OFF + SSD_BC
SC_OFF = SDT_OFF + 2 * SSD_HEADS
Q_OFF = SC_OFF + SSD_BC
SZ_OFF = Q_OFF + DIFF_QW
FN_OFF = SZ_OFF + SSD_W
CF_OFF = FN_OFF + FNET_W
GATE_OFF = CF_OFF + 2 * CONF_W
N_IN = GATE_OFF + N_BRANCH * D_MODEL

A_K = 0
A_V = A_K + DIFF_QW
A_XBC = A_V + DIFF_W
A_W = A_XBC + SSD_W + 2 * SSD_BC
M_CF = 0
M_Q = M_CF + 2 * CONF_W
M_SZ = M_Q + DIFF_QW
M_FN = M_SZ + SSD_W
M_W = M_FN + FNET_W
W_STATE = 0
W_MIX = W_STATE + A_W
W_GATE = W_MIX + M_W
W_DT = W_GATE + N_BRANCH * D_MODEL
W_ALL = W_DT + 128

N_LAT = BATCH * SEQ
N_CTX = BATCH * CTX_LEN
N_ROWS = N_LAT + N_CTX
N_KEYS = SEQ + CTX_LEN

VMEM_LIMIT_BYTES = 56 * 1024 * 1024
LANE = 128
ROW_TILE = 512
COL_TILE = 512
PROJ_COL_TILE = 1024
ATTN_Q_TILE = 256
MOE_TILE = 256
MOE_ROWS_MAX = None


def _segment_of_row_tile(i, tm):
    return jnp.minimum(i * tm // SEQ, BATCH)


def _ws_kernel(*refs, n_a, a_of_w, n_extra, epilogue):
    n_w = len(a_of_w)
    a_refs = refs[:n_a]
    w_refs = refs[n_a:n_a + n_w]
    extra_refs = refs[n_a + n_w:n_a + n_w + n_extra]
    o_ref = refs[n_a + n_w + n_extra]
    scratch_refs = list(refs[n_a + n_w + n_extra + 1:])
    wb_refs = [w_ref if w_ref.dtype == BF16 else scratch_refs.pop(0) for w_ref in w_refs]

    @pl.when(pl.program_id(1) == 0)
    def _():
        for w_ref, wb_ref in zip(w_refs, wb_refs):
            if w_ref is not wb_ref:
                wb_ref[...] = w_ref[...].astype(BF16)

    a_vals = [a_ref[...] for a_ref in a_refs]
    dots = [jnp.dot(a_vals[ai], wb_ref[...], preferred_element_type=F32)
            for ai, wb_ref in zip(a_of_w, wb_refs)]
    o_ref[...] = epilogue(dots, [e[...] for e in extra_refs]).astype(o_ref.dtype)


def ws_matmul(a_list, w_list, a_of_w, epilogue, extras=(), *, m, n, out_dtype, name, tm=ROW_TILE, tn=COL_TILE):
    assert m % tm == 0 and n % tn == 0
    in_specs = []
    operands = []
    for a in a_list:
        if isinstance(a, tuple):
            a, width, cb = a
            in_specs.append(pl.BlockSpec((tm, width), lambda j, i, cb=cb: (i, cb)))
        else:
            in_specs.append(pl.BlockSpec((tm, a.shape[1]), lambda j, i: (i, 0)))
        operands.append(a)
    scratch = []
    for w, prefix, colfn in w_list:
        kdim = w.shape[-2]
        in_specs.append(pl.BlockSpec((None,) * len(prefix) + (kdim, tn),
                                     lambda j, i, prefix=prefix, colfn=colfn: prefix + (0, colfn(j))))
        operands.append(w)
        if w.dtype != BF16:
            scratch.append(pltpu.VMEM((kdim, tn), BF16))
    for ex in extras:
        if ex[0] == 'tile':
            _, arr, colfn = ex
            in_specs.append(pl.BlockSpec((tm, tn), lambda j, i, colfn=colfn: (i, colfn(j))))
        else:
            _, arr = ex
            in_specs.append(pl.BlockSpec((None, 1, tn), lambda j, i: (_segment_of_row_tile(i, tm), 0, j)))
        operands.append(arr)
    kern = functools.partial(_ws_kernel, n_a=len(a_list), a_of_w=tuple(a_of_w), n_extra=len(extras),
                             epilogue=epilogue)
    return pl.pallas_call(
        kern,
        name=name,
        grid=(n // tn, m // tm),
        in_specs=in_specs,
        out_specs=pl.BlockSpec((tm, tn), lambda j, i: (i, j)),
        out_shape=jax.ShapeDtypeStruct((m, n), out_dtype),
        scratch_shapes=scratch,
        compiler_params=pltpu.CompilerParams(
            dimension_semantics=("parallel", "arbitrary"), vmem_limit_bytes=VMEM_LIMIT_BYTES),
    )(*operands)


def _ident(dots, extras):
    return dots[0]


def project(h, w, col_start, n, *, m, name, prefix=(), out_dtype=BF16, tn=COL_TILE):
    assert col_start % tn == 0
    off = col_start // tn
    return ws_matmul([h], [(w, prefix, lambda j: j + off)], [0], _ident, m=m, n=n, out_dtype=out_dtype, tn=tn,
                     name=name)


def _ada_kernel(c_ref, w_ref, b_ref, o_ref):
    o_ref[...] = jnp.dot(c_ref[...], w_ref[...].astype(BF16), preferred_element_type=F32) + b_ref[...]


def ada_modulation(cc, w_ada, b_ada, tn=COL_TILE):
    nl, d, n = w_ada.shape
    return pl.pallas_call(
        _ada_kernel,
        name="ada_modulation",
        grid=(nl, n // tn),
        in_specs=[pl.BlockSpec((16, d), lambda l, j: (0, 0)),
                  pl.BlockSpec((None, d, tn), lambda l, j: (l, 0, j)),
                  pl.BlockSpec((None, 1, tn), lambda l, j: (l, 0, j))],
        out_specs=pl.BlockSpec((None, 16, tn), lambda l, j: (l, 0, j)),
        out_shape=jax.ShapeDtypeStruct((nl, 16, n), F32),
        compiler_params=pltpu.CompilerParams(
            dimension_semantics=("parallel", "parallel"), vmem_limit_bytes=VMEM_LIMIT_BYTES),
    )(cc, w_ada, b_ada.reshape(nl, 1, n))


def _rope(u, cos, sin_signed, first_half):
    partner = jnp.where(first_half, pltpu.roll(u, LANE - ROPE_FREQS, axis=1), pltpu.roll(u, ROPE_FREQS, axis=1))
    return u * cos + partner * sin_signed


def _attn_kernel(lam_ref, q_ref, k_ref, v_ref, cq_ref, sq_ref, ck_ref, sk_ref, g_ref, o_ref, k_s,
                 *, out_scale):
    lane = lax.broadcasted_iota(jnp.int32, (1, LANE), 1)
    first_half = (lane % (2 * ROPE_FREQS)) < ROPE_FREQS
    comp0 = lane < DIFF_DK

    @pl.when(pl.program_id(2) == 0)
    def _():
        k_s[...] = _rope(k_ref[...].astype(F32), ck_ref[...], sk_ref[...], first_half).astype(BF16)

    q = _rope(q_ref[...].astype(F32), cq_ref[...], sq_ref[...], first_half) * (DIFF_SCALE * math.log2(math.e))
    k = k_s[...]
    lam = lam_ref[0]
    unnorm, denom = [], []
    for c in range(2):
        qc = jnp.where(comp0 if c == 0 else jnp.logical_not(comp0), q, 0.0).astype(BF16)
        s = lax.dot_general(qc, k, (((1,), (1,)), ((), ())), preferred_element_type=F32)
        p = jnp.exp2(s - jnp.max(s, axis=-1, keepdims=True))
        unnorm.append(p)
        denom.append(jnp.sum(p, axis=-1, keepdims=True))
    a = (unnorm[0] - unnorm[1] * (lam * denom[0] / denom[1])).astype(BF16)
    o = jnp.dot(a, v_ref[...], preferred_element_type=F32) / denom[0]
    o = o * lax.rsqrt(jnp.mean(o * o, axis=-1, keepdims=True) + EPS) * (g_ref[...] * out_scale)
    o_ref[...] = o.astype(o_ref.dtype)


def diff_attention(q, kv_all, rope_q, rope_k, lam, g_subln, lam_init, *, q_col_block):
    bsz, lq = q.shape[:2]
    keys = kv_all.shape[1]
    tq = min(ATTN_Q_TILE, lq)
    cq, sq = rope_q
    ck, sk = rope_k
    const = dict(pipeline_mode=pl.Buffered(1))
    kern = functools.partial(_attn_kernel, out_scale=1.0 - lam_init)
    return pl.pallas_call(
        kern,
        name="diff_attention",
        grid=(bsz, DIFF_HEADS, lq // tq),
        in_specs=[pl.BlockSpec(memory_space=pltpu.SMEM),
                  pl.BlockSpec((None, tq, LANE), lambda b, h, i: (b, i, q_col_block + h)),
                  pl.BlockSpec((None, keys, LANE), lambda b, h, i: (b, 0, h)),
                  pl.BlockSpec((None, keys, LANE), lambda b, h, i: (b, 0, DIFF_HEADS + h)),
                  pl.BlockSpec((tq, LANE), lambda b, h, i: (i, 0)),
                  pl.BlockSpec((tq, LANE), lambda b, h, i: (i, 0)),
                  pl.BlockSpec((keys, LANE), lambda b, h, i: (0, 0), **const),
                  pl.BlockSpec((keys, LANE), lambda b, h, i: (0, 0), **const),
                  pl.BlockSpec((1, LANE), lambda b, h, i: (0, 0))],
        out_specs=pl.BlockSpec((None, tq, LANE), lambda b, h, i: (b, i, h)),
        out_shape=jax.ShapeDtypeStruct((bsz, lq, DIFF_W), BF16),
        scratch_shapes=[pltpu.VMEM((keys, LANE), BF16)],
        compiler_params=pltpu.CompilerParams(
            dimension_semantics=("parallel", "parallel", "arbitrary"), vmem_limit_bytes=VMEM_LIMIT_BYTES),
    )(lam.reshape(1).astype(F32), q, kv_all, kv_all, cq, sq, ck, sk, g_subln.reshape(1, LANE).astype(F32))


def rope_tables(s):
    pos = jnp.arange(s, dtype=jnp.int32)
    row = (pos // GRID_W).astype(F32)[:, None]
    col = (pos % GRID_W).astype(F32)[:, None]
    lane = jnp.arange(LANE, dtype=jnp.int32)
    d = lane % DIFF_DK
    inv = ROPE_BASE ** (-(d % ROPE_FREQS).astype(F32) / ROPE_FREQS)
    ang = jnp.where((d < DIFF_DK // 2)[None, :], row * inv[None, :], col * inv[None, :])
    sign = jnp.where((d % (2 * ROPE_FREQS)) < ROPE_FREQS, -1.0, 1.0)
    return jnp.cos(ang), jnp.sin(ang) * sign[None, :]


HALF_D = D_MODEL // 2
HI_MASK = 0xFFFF0000
ROW_BITS = 14
assert N_ROWS <= 1 << ROW_BITS


def _pack_halves(y):
    n = y.shape[1] // 2
    lo = lax.bitcast_convert_type(y[:, :n].astype(BF16).astype(F32), jnp.uint32)
    hi = lax.bitcast_convert_type(y[:, n:].astype(BF16).astype(F32), jnp.uint32)
    return (lo >> 16) | (hi & jnp.uint32(HI_MASK))


def _unpack_halves(w):
    lo = lax.bitcast_convert_type(w << 16, F32)
    hi = lax.bitcast_convert_type(w & jnp.uint32(HI_MASK), F32)
    return lo, hi


def _moe_kernel(te_ref, nu_ref, r0_ref, tv_ref, plan_ref, x_hbm, wgu_ref, wdn_ref, y_hbm,
                xbuf, ybuf, pad_s, wgu_s, wdn_s, gsem, ssem):
    t = pl.program_id(0)
    n_used = nu_ref[0]
    slot = t % 2
    active = t < n_used

    def gather_rows(tile, dst_slot):
        base = r0_ref[tile]
        for r in range(MOE_TILE):
            src = plan_ref[base + r] & ((1 << ROW_BITS) - 1)
            pltpu.make_async_copy(x_hbm.at[pl.ds(src, 1)], xbuf.at[dst_slot, pl.ds(r, 1)],
                                  gsem.at[dst_slot]).start()

    def scatter_rows(tile, src_slot):
        base = r0_ref[tile]
        n_valid = tv_ref[tile]

        def body(r, carry):
            dst = plan_ref[base + r] >> ROW_BITS
            pltpu.make_async_copy(ybuf.at[src_slot, pl.ds(r, 1)], y_hbm.at[pl.ds(dst, 1)],
                                  ssem.at[src_slot]).start()
            return carry

        def pad_body(r, carry):
            pltpu.make_async_copy(ybuf.at[src_slot, pl.ds(r, 1)], pad_s.at[src_slot, pl.ds(r, 1)],
                                  ssem.at[src_slot]).start()
            return carry

        @pl.when(n_valid == MOE_TILE)
        def _():
            for r in range(MOE_TILE):
                body(r, 0)

        @pl.when(n_valid < MOE_TILE)
        def _():
            lax.fori_loop(0, n_valid, body, 0)
            lax.fori_loop(n_valid, MOE_TILE, pad_body, 0)

    def wait_gather(s):
        pltpu.make_async_copy(x_hbm.at[pl.ds(0, MOE_TILE)], xbuf.at[s], gsem.at[s]).wait()

    def wait_scatter(s):
        pltpu.make_async_copy(ybuf.at[s], y_hbm.at[pl.ds(0, MOE_TILE)], ssem.at[s]).wait()

    @pl.when(t == 0)
    def _():
        gather_rows(0, 0)

    @pl.when(jnp.logical_and(t >= 2, t - 2 < n_used))
    def _():
        wait_scatter(slot)

    prev = te_ref[jnp.maximum(t - 1, 0)]

    @pl.when(jnp.logical_or(t == 0, te_ref[t] != prev))
    def _():
        wgu_s[...] = wgu_ref[...].astype(BF16)
        wdn_s[...] = wdn_ref[...].astype(BF16)

    @pl.when(active)
    def _():
        wait_gather(slot)
        gather_rows(jnp.minimum(t + 1, n_used - 1), 1 - slot)
        lo, hi = _unpack_halves(xbuf[slot])
        gu = (jnp.dot(lo.astype(BF16), wgu_s[:HALF_D], preferred_element_type=F32)
              + jnp.dot(hi.astype(BF16), wgu_s[HALF_D:], preferred_element_type=F32))
        gate = gu[:, :EXPERT_W]
        act = (gate * jax.nn.sigmoid(gate) * gu[:, EXPERT_W:]).astype(BF16)
        ybuf[slot] = _pack_halves(jnp.dot(act, wdn_s[...], preferred_element_type=F32))
        scatter_rows(t, slot)

    @pl.when(t == pl.num_programs(0) - 1)
    def _():
        wait_gather(n_used % 2)

        @pl.when(jnp.logical_and(t >= 1, t - 1 < n_used))
        def _():
            wait_scatter(1 - slot)


def moe_grouped(x_packed, tile_expert, n_used, tile_row0, tile_valid, plan, w_gu, w_down, layer):
    n_tok = x_packed.shape[0]
    n_tiles = tile_expert.shape[0]
    n_out = n_tok * TOP_K
    buf = pltpu.VMEM((2, MOE_TILE, HALF_D), jnp.uint32)
    wmap = lambda t, te, nu, r0, tv, plan: (layer, te[t], 0, 0)
    return pl.pallas_call(
        _moe_kernel,
        name="moe_experts",
        grid_spec=pltpu.PrefetchScalarGridSpec(
            num_scalar_prefetch=5,
            grid=(n_tiles,),
            in_specs=[pl.BlockSpec(memory_space=pl.ANY),
                      pl.BlockSpec((None, None, D_MODEL, 2 * EXPERT_W), wmap),
                      pl.BlockSpec((None, None, EXPERT_W, D_MODEL), wmap)],
            out_specs=pl.BlockSpec(memory_space=pl.ANY),
            scratch_shapes=[buf, buf, buf,
                            pltpu.VMEM((D_MODEL, 2 * EXPERT_W), BF16),
                            pltpu.VMEM((EXPERT_W, D_MODEL), BF16),
                            pltpu.SemaphoreType.DMA((2,)),
                            pltpu.SemaphoreType.DMA((2,))]),
        out_shape=jax.ShapeDtypeStruct((n_out, HALF_D), jnp.uint32),
        compiler_params=pltpu.CompilerParams(
            dimension_semantics=("arbitrary",), vmem_limit_bytes=VMEM_LIMIT_BYTES),
    )(tile_expert, n_used, tile_row0, tile_valid, plan, x_packed, w_gu, w_down)


def _combine_kernel(w_ref, *refs):
    y_refs, o_ref = refs[:TOP_K], refs[TOP_K]
    w = w_ref[...]
    acc_lo = acc_hi = None
    for k in range(TOP_K):
        lo, hi = _unpack_halves(y_refs[k][...])
        wk = w[:, k:k + 1]
        acc_lo = lo * wk if acc_lo is None else acc_lo + lo * wk
        acc_hi = hi * wk if acc_hi is None else acc_hi + hi * wk
    o_ref[:, :HALF_D] = acc_lo.astype(o_ref.dtype)
    o_ref[:, HALF_D:] = acc_hi.astype(o_ref.dtype)


def moe_combine(y, w, t, tm=256):
    blocks = t // tm
    return pl.pallas_call(
        _combine_kernel,
        name="moe_combine",
        grid=(blocks,),
        in_specs=[pl.BlockSpec((tm, TOP_K), lambda i: (i, 0))]
                 + [pl.BlockSpec((tm, HALF_D), lambda i, k=k: (k * blocks + i, 0)) for k in range(TOP_K)],
        out_specs=pl.BlockSpec((tm, D_MODEL), lambda i: (i, 0)),
        out_shape=jax.ShapeDtypeStruct((t, D_MODEL), BF16),
        compiler_params=pltpu.CompilerParams(
            dimension_semantics=("parallel",), vmem_limit_bytes=VMEM_LIMIT_BYTES),
    )(w, *([y] * TOP_K))


def route(h, w_router, router_bias):
    logits = jnp.dot(h.astype(F32), w_router, precision=lax.Precision.HIGHEST)
    scores = jax.nn.sigmoid(logits)
    sel = scores + router_bias
    grp = sel.reshape(-1, N_EXPERT_GROUPS, N_EXPERTS // N_EXPERT_GROUPS)
    grp_score = jnp.sum(lax.top_k(grp, 2)[0], axis=-1)
    _, gidx = lax.top_k(grp_score, TOPK_GROUPS)
    gmask = jnp.any(gidx[..., :, None] == jnp.arange(N_EXPERT_GROUPS), axis=-2)
    emask = jnp.repeat(gmask, N_EXPERTS // N_EXPERT_GROUPS, axis=-1)
    _, eidx = lax.top_k(jnp.where(emask, sel, -jnp.inf), TOP_K)
    w = jnp.take_along_axis(scores, eidx, axis=-1)
    w = w / jnp.sum(w, axis=-1, keepdims=True) * ROUTED_SCALE
    return eidx, w


def dispatch_plan(eidx):
    t = eidx.shape[0]
    n_assign = eidx.size
    n_tiles = n_assign // MOE_TILE + N_EXPERTS
    eflat = eidx.reshape(n_assign).astype(jnp.int32)
    _, order = lax.sort_key_val(eflat, jnp.arange(n_assign, dtype=jnp.int32))
    counts = jnp.sum((eflat[:, None] == jnp.arange(N_EXPERTS, dtype=jnp.int32)[None, :]).astype(jnp.int32), axis=0)
    tiles_per = (counts + MOE_TILE - 1) // MOE_TILE
    tile_end = jnp.cumsum(tiles_per)
    tile_start = tile_end - tiles_per
    grp_start = jnp.cumsum(counts) - counts
    n_used = tile_end[-1:]
    tile_ids = jnp.arange(n_tiles, dtype=jnp.int32)
    tile_expert = jnp.sum((tile_ids[:, None] >= tile_end[None, :]).astype(jnp.int32), axis=1)
    tile_expert = jnp.minimum(tile_expert, tile_expert[n_used[0] - 1])
    in_expert = (tile_ids - tile_start[tile_expert]) * MOE_TILE
    used = tile_ids < n_used[0]
    tile_row0 = jnp.where(used, grp_start[tile_expert] + in_expert, 0)
    tile_valid = jnp.where(used, jnp.clip(counts[tile_expert] - in_expert, 0, MOE_TILE), 0)
    token = order // TOP_K
    plan = (((order % TOP_K) * t + token) << ROW_BITS) | token
    filler = jnp.arange(MOE_TILE, dtype=jnp.int32) % t
    return jnp.concatenate([plan, filler]), tile_expert, tile_row0, tile_valid, n_used


def routed_experts(h, h_packed, p, layer):
    t = h.shape[0]
    eidx, w = route(h, p['w_router'], p['router_bias'])
    plan, tile_expert, tile_row0, tile_valid, n_used = dispatch_plan(eidx)
    y = moe_grouped(h_packed, tile_expert, n_used, tile_row0, tile_valid, plan, p['w_exp_gu'], p['w_exp_down'],
                    layer)
    return moe_combine(y, w, t)


def dft_tables(l, scale):
    f = 1
    while f * f < l:
        f *= 2
    assert l % f == 0
    k = jnp.arange(l, dtype=jnp.int32)[None, :]
    j1 = jnp.arange(l // f, dtype=jnp.int32)[:, None]
    j2 = jnp.arange(f, dtype=jnp.int32)[:, None]
    alpha = (2.0 * math.pi / l) * ((j1 * f * k) % l).astype(F32)
    beta = (2.0 * math.pi / l) * ((j2 * k) % l).astype(F32)
    ca, sa, cb, sb = jnp.cos(alpha)[:, None], jnp.sin(alpha)[:, None], jnp.cos(beta)[None], jnp.sin(beta)[None]
    cos = (ca * cb - sa * sb).reshape(l, l) * scale
    sin = (sa * cb + ca * sb).reshape(l, l) * scale
    return cos, sin


def channel_dft_weight():
    cos, sin = dft_tables(FNET_GW, 1.0)
    eye = jnp.eye(FNET_GROUPS, dtype=F32)
    return jnp.concatenate([jnp.kron(eye, cos), jnp.kron(eye, sin)], axis=1)


def _dft_seq_kernel(c_ref, s_ref, p_ref, q_ref, o_ref):
    o_ref[...] = (jnp.dot(c_ref[...], p_ref[...], preferred_element_type=F32)
                  - jnp.dot(s_ref[...], q_ref[...], preferred_element_type=F32)).astype(o_ref.dtype)


def fourier_mix(pq, tables, *, row_start, l, n_seq, tm=ROW_TILE, tn=COL_TILE):
    cos, sin = tables
    tm = min(tm, l)
    seq0 = row_start // l
    nj = FNET_W // tn
    return pl.pallas_call(
        _dft_seq_kernel,
        name="fourier_seq",
        grid=(n_seq, nj, l // tm),
        in_specs=[pl.BlockSpec((tm, l), lambda b, j, i: (i, 0)),
                  pl.BlockSpec((tm, l), lambda b, j, i: (i, 0)),
                  pl.BlockSpec((l, tn), lambda b, j, i: (seq0 + b, j)),
                  pl.BlockSpec((l, tn), lambda b, j, i: (seq0 + b, nj + j))],
        out_specs=pl.BlockSpec((tm, tn), lambda b, j, i: (b * (l // tm) + i, j)),
        out_shape=jax.ShapeDtypeStruct((n_seq * l, FNET_W), BF16),
        compiler_params=pltpu.CompilerParams(
            dimension_semantics=("parallel", "parallel", "arbitrary"), vmem_limit_bytes=VMEM_LIMIT_BYTES),
    )(cos, sin, pq, pq)


CONF_HALO = 16


CONF_ROWS = 64


def _conformer_kernel(prev_ref, a_ref, g_ref, next_ref, pg_ref, ng_ref, w_ref, b_ref, lg_ref, lb_ref,
                      o_ref, v_s, c_s, *, tl):
    i = pl.program_id(1)
    last = pl.num_programs(1) - 1

    def glu(a, g):
        a = a.astype(F32)
        return a * jax.nn.sigmoid(g.astype(F32))

    v_s[0:CONF_HALO] = jnp.where(i > 0, glu(prev_ref[...], pg_ref[...]), 0.0)
    v_s[CONF_HALO:CONF_HALO + tl] = glu(a_ref[...], g_ref[...])
    v_s[CONF_HALO + tl:] = jnp.where(i < last, glu(next_ref[...], ng_ref[...]), 0.0)
    for c in range(CONF_W // LANE):
        lanes = slice(c * LANE, (c + 1) * LANE)
        w = w_ref[:, lanes]
        for rb in range(tl // CONF_ROWS):
            acc = jnp.zeros((CONF_ROWS, LANE), F32) + b_ref[:, lanes]
            for k in range(CONF_K):
                off = rb * CONF_ROWS + CONF_HALO - CONF_K // 2 + k
                acc = acc + v_s[off:off + CONF_ROWS, lanes] * w[k:k + 1]
            c_s[rb * CONF_ROWS:(rb + 1) * CONF_ROWS, lanes] = acc
    acc = c_s[...]
    mu = jnp.mean(acc, axis=-1, keepdims=True)
    cen = acc - mu
    var = jnp.mean(cen * cen, axis=-1, keepdims=True)
    y = cen * lax.rsqrt(var + EPS) * lg_ref[...] + lb_ref[...]
    o_ref[...] = (y * jax.nn.sigmoid(y)).astype(o_ref.dtype)


def conformer_conv(proj, p, *, col_block, row_start, l, n_seq, tl=256):
    tl = min(tl, l)
    nt = l // tl
    hb = tl // CONF_HALO
    first = row_start // CONF_HALO

    def main(c):
        return pl.BlockSpec((tl, CONF_W), lambda b, i: (row_start // tl + b * nt + i, col_block + c))

    def prev(c):
        return pl.BlockSpec((CONF_HALO, CONF_W),
                            lambda b, i: (first + jnp.maximum((b * nt + i) * hb - 1, 0), col_block + c))

    def nxt(c):
        return pl.BlockSpec((CONF_HALO, CONF_W),
                            lambda b, i: (first + jnp.minimum((b * nt + i + 1) * hb, n_seq * nt * hb - 1),
                                          col_block + c))

    vec = pl.BlockSpec((1, CONF_W), lambda b, i: (0, 0))
    kern = functools.partial(_conformer_kernel, tl=tl)
    return pl.pallas_call(
        kern,
        name="conformer_conv",
        grid=(n_seq, nt),
        in_specs=[prev(0), main(0), main(1), nxt(0), prev(1), nxt(1),
                  pl.BlockSpec((CONF_K, CONF_W), lambda b, i: (0, 0)), vec, vec, vec],
        out_specs=pl.BlockSpec((tl, CONF_W), lambda b, i: (b * nt + i, 0)),
        out_shape=jax.ShapeDtypeStruct((n_seq * l, CONF_W), BF16),
        scratch_shapes=[pltpu.VMEM((tl + 2 * CONF_HALO, CONF_W), F32), pltpu.VMEM((tl, CONF_W), F32)],
        compiler_params=pltpu.CompilerParams(
            dimension_semantics=("parallel", "parallel"), vmem_limit_bytes=VMEM_LIMIT_BYTES),
    )(proj, proj, proj, proj, proj, proj, p['conf_conv_w'], p['conf_conv_b'].reshape(1, CONF_W),
      p['conf_ln_g'].reshape(1, CONF_W), p['conf_ln_b'].reshape(1, CONF_W))


SSD_XBC = SSD_W + 2 * SSD_BC
SSD_HALO = 16
SSD_CONV_TILE = 256
HEADS_PER_GROUP = SSD_HEADS // SSD_GROUPS
GROUP_W = HEADS_PER_GROUP * SSD_HD


def _ssd_conv_kernel(prev_ref, x_ref, next_ref, w_ref, b_ref, o_ref, v_s):
    tl = SSD_CONV_TILE
    i = pl.program_id(0)
    lat_tiles = N_LAT // tl
    per_seq = jnp.where(i < lat_tiles, SEQ // tl, CTX_LEN // tl)
    pos = jnp.where(i < lat_tiles, i % (SEQ // tl), (i - lat_tiles) % (CTX_LEN // tl))
    v_s[0:SSD_HALO] = jnp.where(pos > 0, prev_ref[...].astype(F32), 0.0)
    v_s[SSD_HALO:SSD_HALO + tl] = x_ref[...].astype(F32)
    v_s[SSD_HALO + tl:] = jnp.where(pos < per_seq - 1, next_ref[...].astype(F32), 0.0)
    for c in range(SSD_XBC // LANE):
        lanes = slice(c * LANE, (c + 1) * LANE)
        w = w_ref[:, lanes]
        for rb in range(tl // CONF_ROWS):
            acc = jnp.zeros((CONF_ROWS, LANE), F32) + b_ref[:, lanes]
            for k in range(SSD_CONV):
                off = rb * CONF_ROWS + SSD_HALO - SSD_CONV // 2 + k
                acc = acc + v_s[off:off + CONF_ROWS, lanes] * w[k:k + 1]
            o_ref[rb * CONF_ROWS:(rb + 1) * CONF_ROWS, lanes] = (acc * jax.nn.sigmoid(acc)).astype(o_ref.dtype)


def ssd_conv(proj, conv_w, conv_b, *, col_block):
    tl = SSD_CONV_TILE
    hb = tl // SSD_HALO
    n_tiles = N_ROWS // tl
    return pl.pallas_call(
        _ssd_conv_kernel,
        name="ssd_conv",
        grid=(n_tiles,),
        in_specs=[pl.BlockSpec((SSD_HALO, SSD_XBC), lambda i: (jnp.maximum(i * hb - 1, 0), col_block)),
                  pl.BlockSpec((tl, SSD_XBC), lambda i: (i, col_block)),
                  pl.BlockSpec((SSD_HALO, SSD_XBC),
                               lambda i: (jnp.minimum((i + 1) * hb, n_tiles * hb - 1), col_block)),
                  pl.BlockSpec((SSD_CONV, SSD_XBC), lambda i: (0, 0)),
                  pl.BlockSpec((1, SSD_XBC), lambda i: (0, 0))],
        out_specs=pl.BlockSpec((tl, SSD_XBC), lambda i: (i, 0)),
        out_shape=jax.ShapeDtypeStruct((N_ROWS, SSD_XBC), BF16),
        scratch_shapes=[pltpu.VMEM((tl + 2 * SSD_HALO, SSD_XBC), F32)],
        compiler_params=pltpu.CompilerParams(
            dimension_semantics=("parallel",), vmem_limit_bytes=VMEM_LIMIT_BYTES),
    )(proj, proj, proj, conv_w, conv_b.reshape(1, SSD_XBC))


def _split3(v):
    hi = v.astype(BF16)
    r1 = v - hi.astype(F32)
    mid = r1.astype(BF16)
    lo = (r1 - mid.astype(F32)).astype(BF16)
    return hi, mid, lo


def _ssd_scan_kernel(x_ref, b_ref, c_ref, dt_ref, a_ref, bias_ref, y_ref, st, *, reverse):
    @pl.when(pl.program_id(1) == 0)
    def _():
        st[...] = jnp.zeros_like(st)

    t = SSD_CHUNK
    col0 = SSD_HEADS if reverse else 0
    dt_in = dt_ref[:, col0:col0 + SSD_HEADS] + bias_ref[...]
    dt = jnp.maximum(dt_in, 0.0) + jnp.log(1.0 + jnp.exp(-jnp.abs(dt_in)))
    ad = dt * a_ref[...]
    row = lax.broadcasted_iota(jnp.int32, (t, t), 0)
    col = lax.broadcasted_iota(jnp.int32, (t, t), 1)
    keep = (col >= row) if reverse else (col <= row)
    tri = jnp.where(keep, 1.0, 0.0).astype(BF16)
    cs = sum(jnp.dot(tri, piece, preferred_element_type=F32) for piece in _split3(ad))
    cs_t = cs.T
    total = cs[0:1] if reverse else cs[t - 1:t]
    w_end = jnp.exp(total - cs)
    e_cs = jnp.exp(cs)
    e_tot = jnp.exp(total)
    head_of_lane = lax.broadcasted_iota(jnp.int32, (1, GROUP_W), 1) // SSD_HD

    def per_lane(v, g):
        out = v[:, g * HEADS_PER_GROUP:g * HEADS_PER_GROUP + 1]
        for hl in range(1, HEADS_PER_GROUP):
            h = g * HEADS_PER_GROUP + hl
            out = jnp.where(head_of_lane >= hl, v[:, h:h + 1], out)
        return out

    for g in range(SSD_GROUPS):
        bg = b_ref[:, g * SSD_STATE:(g + 1) * SSD_STATE]
        cg = c_ref[:, g * SSD_STATE:(g + 1) * SSD_STATE]
        cols = slice(g * GROUP_W, (g + 1) * GROUP_W)
        gmat = lax.dot_general(cg, bg, (((1,), (1,)), ((), ())), preferred_element_type=F32)
        st_g = st[g]
        xd = x_ref[:, cols].astype(F32) * per_lane(dt, g)
        xd_b = xd.astype(BF16)
        y_g = jnp.dot(cg, st_g.astype(BF16), preferred_element_type=F32) * per_lane(e_cs, g)
        y_diag = None
        for hl in range(HEADS_PER_GROUP):
            h = g * HEADS_PER_GROUP + hl
            lmat = jnp.exp(jnp.where(keep, cs[:, h:h + 1] - cs_t[h:h + 1, :], -jnp.inf))
            r = jnp.dot((gmat * lmat).astype(BF16), xd_b, preferred_element_type=F32)
            y_diag = r if y_diag is None else jnp.where(head_of_lane >= hl, r, y_diag)
        y_ref[:, cols] = (y_g + y_diag).astype(y_ref.dtype)
        upd = lax.dot_general(bg, (xd * per_lane(w_end, g)).astype(BF16), (((0,), (0,)), ((), ())),
                              preferred_element_type=F32)
        st[g] = st_g * per_lane(e_tot, g) + upd


def ssd_scan(u, dt_raw, a_log, dt_bias, *, reverse):
    d = 1 if reverse else 0
    ctx_chunks = CTX_LEN // SSD_CHUNK
    lat_chunks = SEQ // SSD_CHUNK
    n_chunks = ctx_chunks + lat_chunks

    def rows(b, s):
        ctx_i = (ctx_chunks - 1 - s) if reverse else s
        lat_i = (n_chunks - 1 - s) if reverse else (s - ctx_chunks)
        return jnp.where(s < ctx_chunks, N_LAT // SSD_CHUNK + b * ctx_chunks + ctx_i, b * lat_chunks + lat_i)

    xw = SSD_W // SSD_BC
    kern = functools.partial(_ssd_scan_kernel, reverse=reverse)
    return pl.pallas_call(
        kern,
        name="ssd_scan",
        grid=(BATCH, n_chunks),
        in_specs=[pl.BlockSpec((SSD_CHUNK, SSD_W), lambda b, s: (rows(b, s), 0)),
                  pl.BlockSpec((SSD_CHUNK, SSD_BC), lambda b, s: (rows(b, s), xw)),
                  pl.BlockSpec((SSD_CHUNK, SSD_BC), lambda b, s: (rows(b, s), xw + 1)),
                  pl.BlockSpec((SSD_CHUNK, LANE), lambda b, s: (rows(b, s), 0)),
                  pl.BlockSpec((1, SSD_HEADS), lambda b, s: (0, 0)),
                  pl.BlockSpec((1, SSD_HEADS), lambda b, s: (0, 0))],
        out_specs=pl.BlockSpec((SSD_CHUNK, SSD_W), lambda b, s: (rows(b, s), 0)),
        out_shape=jax.ShapeDtypeStruct((N_ROWS, SSD_W), BF16),
        scratch_shapes=[pltpu.VMEM((SSD_GROUPS, SSD_STATE, GROUP_W), F32)],
        compiler_params=pltpu.CompilerParams(
            dimension_semantics=("parallel", "arbitrary"), vmem_limit_bytes=VMEM_LIMIT_BYTES),
    )(u, u, u, dt_raw, -jnp.exp(a_log[d]).reshape(1, SSD_HEADS), dt_bias[d].reshape(1, SSD_HEADS))


def _ssd_finish_kernel(yf_ref, yb_ref, x_ref, z_ref, d_ref, g_ref, o_ref):
    z = z_ref[...].astype(F32)
    y = yf_ref[...].astype(F32) + yb_ref[...].astype(F32) + d_ref[...] * x_ref[...].astype(F32)
    y = y * (z * jax.nn.sigmoid(z))
    gw = SSD_W // SSD_GROUPS
    for g in range(SSD_GROUPS):
        cols = slice(g * gw, (g + 1) * gw)
        yg = y[:, cols]
        yg = yg * lax.rsqrt(jnp.mean(yg * yg, axis=-1, keepdims=True) + EPS)
        o_ref[:, cols] = (yg * g_ref[:, cols]).astype(o_ref.dtype)


def ssd_finish(y_f, y_b, u, proj_mix, ssd_d, norm_g, *, m, z_col_block, tm=ROW_TILE):
    row = pl.BlockSpec((tm, SSD_W), lambda i: (i, 0))
    vec = pl.BlockSpec((1, SSD_W), lambda i: (0, 0))
    return pl.pallas_call(
        _ssd_finish_kernel,
        name="ssd_finish",
        grid=(m // tm,),
        in_specs=[row, row, row, pl.BlockSpec((tm, SSD_W), lambda i: (i, z_col_block)), vec, vec],
        out_specs=row,
        out_shape=jax.ShapeDtypeStruct((m, SSD_W), BF16),
        compiler_params=pltpu.CompilerParams(
            dimension_semantics=("parallel",), vmem_limit_bytes=VMEM_LIMIT_BYTES),
    )(y_f, y_b, u, proj_mix, jnp.repeat(ssd_d, SSD_HD).reshape(1, SSD_W), norm_g.reshape(1, SSD_W))


def _lat(a):
    return a[:N_LAT].reshape(BATCH, SEQ, a.shape[-1])


def _ctx(a):
    return a[N_LAT:N_ROWS].reshape(BATCH, CTX_LEN, a.shape[-1])


def _rows(lat, ctx=None):
    lat = lat.reshape(N_LAT, lat.shape[-1])
    if ctx is None:
        return lat
    return jnp.concatenate([lat, ctx.reshape(N_CTX, ctx.shape[-1])], axis=0)


def _merge_epilogue(dots, extras):
    acc = None
    for d, g in zip(dots, extras):
        term = jax.nn.sigmoid(g.astype(F32)) * d
        acc = term if acc is None else acc + term
    return acc


def _residual_epilogue(dots, extras):
    x, g = extras
    return x + g * dots[0]


def _ffn_out_epilogue(dots, extras):
    x, routed, g = extras
    return x + g * (dots[0] + routed)


def _glu_epilogue(dots, extras):
    gate, up = dots
    return gate * jax.nn.sigmoid(gate) * up


def _norm_kernel(x_ref, g_ref, sc_ref, sh_ref, o_ref, *packed_ref):
    x = x_ref[...]
    y = x * lax.rsqrt(jnp.mean(x * x, axis=-1, keepdims=True) + EPS) * g_ref[...]
    y = y * (1.0 + sc_ref[...]) + sh_ref[...]
    o_ref[...] = y.astype(o_ref.dtype)
    if packed_ref:
        packed_ref[0][...] = _pack_halves(y)


def rms_modulate(x, g, scale, shift, *, m, out_dtype=BF16, packed=False, tm=256):
    d = x.shape[1]
    seg_spec = pl.BlockSpec((None, 1, d), lambda i: (_segment_of_row_tile(i, tm), 0, 0))
    out_specs = [pl.BlockSpec((tm, d), lambda i: (i, 0))]
    out_shape = [jax.ShapeDtypeStruct((m, d), out_dtype)]
    if packed:
        out_specs.append(pl.BlockSpec((tm, d // 2), lambda i: (i, 0)))
        out_shape.append(jax.ShapeDtypeStruct((m, d // 2), jnp.uint32))
    outs = pl.pallas_call(
        _norm_kernel,
        name="rms_modulate",
        grid=(m // tm,),
        in_specs=[pl.BlockSpec((tm, d), lambda i: (i, 0)),
                  pl.BlockSpec((1, d), lambda i: (0, 0)),
                  seg_spec, seg_spec],
        out_specs=out_specs,
        out_shape=out_shape,
        compiler_params=pltpu.CompilerParams(
            dimension_semantics=("parallel",), vmem_limit_bytes=VMEM_LIMIT_BYTES),
    )(x, g.reshape(1, d), scale.reshape(-1, 1, d), shift.reshape(-1, 1, d))
    return outs if packed else outs[0]


def token_mixers(h, p, layer_idx, ctx_out, m_full):
    w_all = p['w_all']
    lp = (layer_idx,)
    proj_a = project(h, w_all, W_STATE, A_W, m=N_ROWS, prefix=lp, tn=PROJ_COL_TILE, name="proj_state")
    proj_dt = project(h, w_all, W_DT, LANE, m=N_ROWS, prefix=lp, out_dtype=F32, tn=LANE, name="proj_dt")
    proj_b = project(h, w_all, W_MIX, M_W, m=m_full, prefix=lp, tn=PROJ_COL_TILE, name="proj_mix")
    proj_g = project(h, w_all, W_GATE, N_BRANCH * D_MODEL, m=m_full, prefix=lp, tn=PROJ_COL_TILE,
                     name="proj_gate")

    lam_init = 0.8 - 0.6 * math.exp(-0.3 * layer_idx)
    lam = (jnp.exp(jnp.sum(p['lam_q1'] * p['lam_k1'])) - jnp.exp(jnp.sum(p['lam_q2'] * p['lam_k2'])) + lam_init)
    cos_l, sin_l = rope_tables(SEQ)
    cos_c, sin_c = jnp.ones((CTX_LEN, LANE), F32), jnp.zeros((CTX_LEN, LANE), F32)
    kv_lat = _lat(proj_a[:, :A_XBC])
    kv_ctx = _ctx(proj_a[:, :A_XBC])
    kv_all = jnp.concatenate([kv_lat, kv_ctx], axis=1)
    rope_k = (jnp.concatenate([cos_l, cos_c]), jnp.concatenate([sin_l, sin_c]))
    y_diff = diff_attention(_lat(proj_b), kv_all, (cos_l, sin_l), rope_k, lam, p['g_subln'], lam_init,
                            q_col_block=M_Q // LANE)
    if ctx_out:
        yc_diff = diff_attention(_ctx(proj_b), kv_ctx, (cos_c, sin_c), (cos_c, sin_c), lam, p['g_subln'],
                                 lam_init, q_col_block=M_Q // LANE)

    u = ssd_conv(proj_a, p['ssd_conv_w'], p['ssd_conv_b'], col_block=A_XBC // SSD_XBC)
    y_f = ssd_scan(u, proj_dt, p['ssd_a_log'], p['ssd_dt_bias'], reverse=False)
    y_b = ssd_scan(u, proj_dt, p['ssd_a_log'], p['ssd_dt_bias'], reverse=True)
    y_ssd = ssd_finish(y_f, y_b, u, proj_b, p['ssd_d'], p['ssd_norm_g'], m=m_full, z_col_block=M_SZ // SSD_W)

    pq = ws_matmul([(proj_b, FNET_W, M_FN // FNET_W)], [(p['w_chan_dft'], (), lambda j: j)], [0], _ident,
                   m=m_full, n=2 * FNET_W, out_dtype=BF16, name="fourier_chan")
    y_fn = fourier_mix(pq, p['dft_lat'], row_start=0, l=SEQ, n_seq=BATCH)
    y_cf = conformer_conv(proj_b, p, col_block=M_CF // CONF_W, row_start=0, l=SEQ, n_seq=BATCH)

    if ctx_out:
        yc_fn = fourier_mix(pq, p['dft_ctx'], row_start=N_LAT, l=CTX_LEN, n_seq=BATCH)
        yc_cf = conformer_conv(proj_b, p, col_block=M_CF // CONF_W, row_start=N_LAT, l=CTX_LEN, n_seq=BATCH)
        branches = [jnp.concatenate([y_fn, yc_fn]), _rows(y_diff, yc_diff), y_ssd, jnp.concatenate([y_cf, yc_cf])]
    else:
        branches = [y_fn, _rows(y_diff), y_ssd, y_cf]
    return branches, proj_g


def trunk_layer(x, mods, p, layer_idx, ctx_out):
    sh1, sc1, g1, sh2, sc2, g2 = jnp.split(mods, 6, axis=-1)
    m_full = N_ROWS if ctx_out else N_LAT
    lp = (layer_idx,)
    h = rms_modulate(x, p['g_mix'], sc1, sh1, m=N_ROWS)
    branches, proj_g = token_mixers(h, p, layer_idx, ctx_out, m_full)

    gate_blocks = D_MODEL // COL_TILE
    acc = ws_matmul(
        branches, [(p['w_branch'], (layer_idx, i), lambda j: j) for i in range(N_BRANCH)], list(range(N_BRANCH)),
        _merge_epilogue,
        extras=[('tile', proj_g, lambda j, i=i: i * gate_blocks + j) for i in range(N_BRANCH)],
        m=m_full, n=D_MODEL, out_dtype=BF16, name="merge_branches")
    x = ws_matmul([acc], [(p['w_out'], lp, lambda j: j)], [0], _residual_epilogue,
                  extras=[('tile', x, lambda j: j), ('seg', g1.reshape(3, 1, D_MODEL))],
                  m=m_full, n=D_MODEL, out_dtype=F32, name="mix_out")

    h2, h2_packed = rms_modulate(x, p['g_ffn'], sc2, sh2, m=m_full, packed=True)
    routed = routed_experts(h2, h2_packed, p, layer_idx)
    up_blocks = SHARED_W // COL_TILE
    act = ws_matmul([h2], [(p['w_sh_gu'], lp, lambda j: j), (p['w_sh_gu'], lp, lambda j: j + up_blocks)], [0, 0],
                    _glu_epilogue, m=m_full, n=SHARED_W, out_dtype=BF16, name="shared_glu")
    x = ws_matmul([act], [(p['w_sh_down'], lp, lambda j: j)], [0], _ffn_out_epilogue,
                  extras=[('tile', x, lambda j: j), ('tile', routed, lambda j: j),
                          ('seg', g2.reshape(3, 1, D_MODEL))],
                  m=m_full, n=D_MODEL, out_dtype=F32, name="ffn_out")
    return x


def kernel(x, c, ctx, c_ctx, w_ada, b_ada, g_mix, w_in, lam_q1, lam_k1, lam_q2, lam_k2, g_subln, ssd_conv_w, ssd_conv_b, ssd_dt_bias, ssd_a_log, ssd_d, ssd_norm_g, conf_conv_w, conf_conv_b, conf_ln_g, conf_ln_b, w_branch, w_out, g_ffn, w_router, router_bias, w_exp_gu, w_exp_down, w_sh_gu, w_sh_down, g_final):
    cc = jnp.concatenate([jax.nn.silu(c), jax.nn.silu(c_ctx)[None]], axis=0)
    cc = jnp.pad(cc, ((0, 16 - cc.shape[0]), (0, 0))).astype(BF16)
    mods = ada_modulation(cc, w_ada, b_ada)[:, :BATCH + 1]

    dft_lat = tuple(t.astype(BF16) for t in dft_tables(SEQ, (SEQ * FNET_GW) ** -0.5))
    dft_ctx = tuple(t.astype(BF16) for t in dft_tables(CTX_LEN, (CTX_LEN * FNET_GW) ** -0.5))
    w_chan_dft = channel_dft_weight()
    w_all = jnp.concatenate(
        [w_in[:, :, K_OFF:SDT_OFF], w_in[:, :, SC_OFF:Q_OFF],
         w_in[:, :, CF_OFF:GATE_OFF], w_in[:, :, Q_OFF:CF_OFF],
         w_in[:, :, GATE_OFF:],
         jnp.pad(w_in[:, :, SDT_OFF:SC_OFF], ((0, 0), (0, 0), (0, W_ALL - W_DT - 2 * SSD_HEADS)))],
        axis=2).astype(BF16)

    xa = jnp.concatenate([x.reshape(N_LAT, D_MODEL), ctx.reshape(N_CTX, D_MODEL)], axis=0)
    for l in range(DEPTH):
        p = {
            'g_mix': g_mix[l], 'w_all': w_all,
            'lam_q1': lam_q1[l], 'lam_k1': lam_k1[l], 'lam_q2': lam_q2[l], 'lam_k2': lam_k2[l],
            'g_subln': g_subln[l], 'ssd_conv_w': ssd_conv_w[l], 'ssd_conv_b': ssd_conv_b[l],
            'ssd_dt_bias': ssd_dt_bias[l], 'ssd_a_log': ssd_a_log[l], 'ssd_d': ssd_d[l],
            'ssd_norm_g': ssd_norm_g[l], 'conf_conv_w': conf_conv_w[l], 'conf_conv_b': conf_conv_b[l],
            'conf_ln_g': conf_ln_g[l], 'conf_ln_b': conf_ln_b[l], 'w_branch': w_branch,
            'w_out': w_out, 'g_ffn': g_ffn[l], 'w_router': w_router[l], 'router_bias': router_bias[l],
            'w_exp_gu': w_exp_gu, 'w_exp_down': w_exp_down, 'w_sh_gu': w_sh_gu,
            'w_sh_down': w_sh_down,
            'dft_lat': dft_lat, 'dft_ctx': dft_ctx, 'w_chan_dft': w_chan_dft,
        }
        xa = trunk_layer(xa, mods[l], p, l, l < DEPTH - 1)
    zeros = jnp.zeros((BATCH + 1, D_MODEL), F32)
    y = rms_modulate(xa, g_final, zeros, zeros, m=N_LAT, out_dtype=F32)
    return y.reshape(BATCH, SEQ, D_MODEL)
```

```python
import functools
import math

import jax
import jax.numpy as jnp
from jax import lax
from jax.experimental import pallas as pl
from jax.experimental.pallas import tpu as pltpu

F32 = jnp.float32
BF16 = jnp.bfloat16

D_MODEL = 4096
BATCH = 2
SEQ = 4096
DEPTH = 2
CTX_LEN = 256
GRID_W = 64
N_BRANCH = 4
EPS = 1e-6
FNET_GROUPS = 4
FNET_GW = 256
FNET_W = FNET_GROUPS * FNET_GW
DIFF_HEADS = 8
DIFF_DK = 64
DIFF_DV = 128
DIFF_QW = DIFF_HEADS * 2 * DIFF_DK
DIFF_W = DIFF_HEADS * DIFF_DV
DIFF_SCALE = DIFF_DK ** -0.5
ROPE_BASE = 10000.0
ROPE_FREQS = DIFF_DK // 4
SSD_HEADS = 16
SSD_HD = 64
SSD_W = SSD_HEADS * SSD_HD
SSD_STATE = 128
SSD_GROUPS = 4
SSD_BC = SSD_GROUPS * SSD_STATE
SSD_CONV = 5
SSD_CHUNK = 128
CONF_W = 1024
CONF_K = 31
N_EXPERTS = 64
TOP_K = 6
N_EXPERT_GROUPS = 8
TOPK_GROUPS = 4
EXPERT_W = 256
SHARED_W = 1024
ROUTED_SCALE = 2.5

K_OFF = 0
V_OFF = K_OFF + DIFF_QW
SX_OFF = V_OFF + DIFF_W
SB_OFF = SX_OFF + SSD_W
SDT_OFF = SB_OFF + SSD_BC
SC_OFF = SDT_OFF + 2 * SSD_HEADS
Q_OFF = SC_OFF + SSD_BC
SZ_OFF = Q_OFF + DIFF_QW
FN_OFF = SZ_OFF + SSD_W
CF_OFF = FN_OFF + FNET_W
GATE_OFF = CF_OFF + 2 * CONF_W
N_IN = GATE_OFF + N_BRANCH * D_MODEL

A_K = 0
A_V = A_K + DIFF_QW
A_XBC = A_V + DIFF_W
A_W = A_XBC + SSD_W + 2 * SSD_BC
M_CF = 0
M_Q = M_CF + 2 * CONF_W
M_SZ = M_Q + DIFF_QW
M_FN = M_SZ + SSD_W
M_W = M_FN + FNET_W
W_STATE = 0
W_MIX = W_STATE + A_W
W_GATE = W_MIX + M_W
W_DT = W_GATE + N_BRANCH * D_MODEL
W_ALL = W_DT + 128

N_LAT = BATCH * SEQ
N_CTX = BATCH * CTX_LEN
N_ROWS = N_LAT + N_CTX
N_KEYS = SEQ + CTX_LEN

VMEM_LIMIT_BYTES = 56 * 1024 * 1024
LANE = 128
ROW_TILE = 512
COL_TILE = 512
PROJ_COL_TILE = 1024
ATTN_Q_TILE = 256
MOE_TILE = 256
MOE_ROWS_MAX = None


def _segment_of_row_tile(i, tm):
    return jnp.minimum(i * tm // SEQ, BATCH)


def _ws_kernel(*refs, n_a, a_of_w, n_extra, epilogue):
    n_w = len(a_of_w)
    a_refs = refs[:n_a]
    w_refs = refs[n_a:n_a + n_w]
    extra_refs = refs[n_a + n_w:n_a + n_w + n_extra]
    o_ref = refs[n_a + n_w + n_extra]
    scratch_refs = list(refs[n_a + n_w + n_extra + 1:])
    wb_refs = [w_ref if w_ref.dtype == BF16 else scratch_refs.pop(0) for w_ref in w_refs]

    @pl.when(pl.program_id(1) == 0)
    def _():
        for w_ref, wb_ref in zip(w_refs, wb_refs):
            if w_ref is not wb_ref:
                wb_ref[...] = w_ref[...].astype(BF16)

    a_vals = [a_ref[...] for a_ref in a_refs]
    dots = [jnp.dot(a_vals[ai], wb_ref[...], preferred_element_type=F32)
            for ai, wb_ref in zip(a_of_w, wb_refs)]
    o_ref[...] = epilogue(dots, [e[...] for e in extra_refs]).astype(o_ref.dtype)


def ws_matmul(a_list, w_list, a_of_w, epilogue, extras=(), *, m, n, out_dtype, name, tm=ROW_TILE, tn=COL_TILE):
    assert m % tm == 0 and n % tn == 0
    in_specs = []
    operands = []
    for a in a_list:
        if isinstance(a, tuple):
            a, width, cb = a
            in_specs.append(pl.BlockSpec((tm, width), lambda j, i, cb=cb: (i, cb)))
        else:
            in_specs.append(pl.BlockSpec((tm, a.shape[1]), lambda j, i: (i, 0)))
        operands.append(a)
    scratch = []
    for w, prefix, colfn in w_list:
        kdim = w.shape[-2]
        in_specs.append(pl.BlockSpec((None,) * len(prefix) + (kdim, tn),
                                     lambda j, i, prefix=prefix, colfn=colfn: prefix + (0, colfn(j))))
        operands.append(w)
        if w.dtype != BF16:
            scratch.append(pltpu.VMEM((kdim, tn), BF16))
    for ex in extras:
        if ex[0] == 'tile':
            _, arr, colfn = ex
            in_specs.append(pl.BlockSpec((tm, tn), lambda j, i, colfn=colfn: (i, colfn(j))))
        else:
            _, arr = ex
            in_specs.append(pl.BlockSpec((None, 1, tn), lambda j, i: (_segment_of_row_tile(i, tm), 0, j)))
        operands.append(arr)
    kern = functools.partial(_ws_kernel, n_a=len(a_list), a_of_w=tuple(a_of_w), n_extra=len(extras),
                             epilogue=epilogue)
    return pl.pallas_call(
        kern,
        name=name,
        grid=(n // tn, m // tm),
        in_specs=in_specs,
        out_specs=pl.BlockSpec((tm, tn), lambda j, i: (i, j)),
        out_shape=jax.ShapeDtypeStruct((m, n), out_dtype),
        scratch_shapes=scratch,
        compiler_params=pltpu.CompilerParams(
            dimension_semantics=("parallel", "arbitrary"), vmem_limit_bytes=VMEM_LIMIT_BYTES),
    )(*operands)


def _ident(dots, extras):
    return dots[0]


def project(h, w, col_start, n, *, m, name, prefix=(), out_dtype=BF16, tn=COL_TILE):
    assert col_start % tn == 0
    off = col_start // tn
    return ws_matmul([h], [(w, prefix, lambda j: j + off)], [0], _ident, m=m, n=n, out_dtype=out_dtype, tn=tn,
                     name=name)


def _ada_kernel(c_ref, w_ref, b_ref, o_ref):
    o_ref[...] = jnp.dot(c_ref[...], w_ref[...].astype(BF16), preferred_element_type=F32) + b_ref[...]


def ada_modulation(cc, w_ada, b_ada, tn=COL_TILE):
    nl, d, n = w_ada.shape
    return pl.pallas_call(
        _ada_kernel,
        name="ada_modulation",
        grid=(nl, n // tn),
        in_specs=[pl.BlockSpec((16, d), lambda l, j: (0, 0)),
                  pl.BlockSpec((None, d, tn), lambda l, j: (l, 0, j)),
                  pl.BlockSpec((None, 1, tn), lambda l, j: (l, 0, j))],
        out_specs=pl.BlockSpec((None, 16, tn), lambda l, j: (l, 0, j)),
        out_shape=jax.ShapeDtypeStruct((nl, 16, n), F32),
        compiler_params=pltpu.CompilerParams(
            dimension_semantics=("parallel", "parallel"), vmem_limit_bytes=VMEM_LIMIT_BYTES),
    )(cc, w_ada, b_ada.reshape(nl, 1, n))


def _rope(u, cos, sin_signed, first_half):
    partner = jnp.where(first_half, pltpu.roll(u, LANE - ROPE_FREQS, axis=1), pltpu.roll(u, ROPE_FREQS, axis=1))
    return u * cos + partner * sin_signed


def _attn_kernel(lam_ref, q_ref, k_ref, v_ref, cq_ref, sq_ref, ck_ref, sk_ref, g_ref, o_ref, k_s,
                 *, out_scale):
    lane = lax.broadcasted_iota(jnp.int32, (1, LANE), 1)
    first_half = (lane % (2 * ROPE_FREQS)) < ROPE_FREQS
    comp0 = lane < DIFF_DK

    @pl.when(pl.program_id(2) == 0)
    def _():
        k_s[...] = _rope(k_ref[...].astype(F32), ck_ref[...], sk_ref[...], first_half).astype(BF16)

    q = _rope(q_ref[...].astype(F32), cq_ref[...], sq_ref[...], first_half) * (DIFF_SCALE * math.log2(math.e))
    k = k_s[...]
    lam = lam_ref[0]
    unnorm, denom = [], []
    for c in range(2):
        qc = jnp.where(comp0 if c == 0 else jnp.logical_not(comp0), q, 0.0).astype(BF16)
        s = lax.dot_general(qc, k, (((1,), (1,)), ((), ())), preferred_element_type=F32)
        p = jnp.exp2(s - jnp.max(s, axis=-1, keepdims=True))
        unnorm.append(p)
        denom.append(jnp.sum(p, axis=-1, keepdims=True))
    a = (unnorm[0] - unnorm[1] * (lam * denom[0] / denom[1])).astype(BF16)
    o = jnp.dot(a, v_ref[...], preferred_element_type=F32) / denom[0]
    o = o * lax.rsqrt(jnp.mean(o * o, axis=-1, keepdims=True) + EPS) * (g_ref[...] * out_scale)
    o_ref[...] = o.astype(o_ref.dtype)


def diff_attention(q, kv_all, rope_q, rope_k, lam, g_subln, lam_init, *, q_col_block):
    bsz, lq = q.shape[:2]
    keys = kv_all.shape[1]
    tq = min(ATTN_Q_TILE, lq)
    cq, sq = rope_q
    ck, sk = rope_k
    const = dict(pipeline_mode=pl.Buffered(1))
    kern = functools.partial(_attn_kernel, out_scale=1.0 - lam_init)
    return pl.pallas_call(
        kern,
        name="diff_attention",
        grid=(bsz, DIFF_HEADS, lq // tq),
        in_specs=[pl.BlockSpec(memory_space=pltpu.SMEM),
                  pl.BlockSpec((None, tq, LANE), lambda b, h, i: (b, i, q_col_block + h)),
                  pl.BlockSpec((None, keys, LANE), lambda b, h, i: (b, 0, h)),
                  pl.BlockSpec((None, keys, LANE), lambda b, h, i: (b, 0, DIFF_HEADS + h)),
                  pl.BlockSpec((tq, LANE), lambda b, h, i: (i, 0)),
                  pl.BlockSpec((tq, LANE), lambda b, h, i: (i, 0)),
                  pl.BlockSpec((keys, LANE), lambda b, h, i: (0, 0), **const),
                  pl.BlockSpec((keys, LANE), lambda b, h, i: (0, 0), **const),
                  pl.BlockSpec((1, LANE), lambda b, h, i: (0, 0))],
        out_specs=pl.BlockSpec((None, tq, LANE), lambda b, h, i: (b, i, h)),
        out_shape=jax.ShapeDtypeStruct((bsz, lq, DIFF_W), BF16),
        scratch_shapes=[pltpu.VMEM((keys, LANE), BF16)],
        compiler_params=pltpu.CompilerParams(
            dimension_semantics=("parallel", "parallel", "arbitrary"), vmem_limit_bytes=VMEM_LIMIT_BYTES),
    )(lam.reshape(1).astype(F32), q, kv_all, kv_all, cq, sq, ck, sk, g_subln.reshape(1, LANE).astype(F32))


def rope_tables(s):
    pos = jnp.arange(s, dtype=jnp.int32)
    row = (pos // GRID_W).astype(F32)[:, None]
    col = (pos % GRID_W).astype(F32)[:, None]
    lane = jnp.arange(LANE, dtype=jnp.int32)
    d = lane % DIFF_DK
    inv = ROPE_BASE ** (-(d % ROPE_FREQS).astype(F32) / ROPE_FREQS)
    ang = jnp.where((d < DIFF_DK // 2)[None, :], row * inv[None, :], col * inv[None, :])
    sign = jnp.where((d % (2 * ROPE_FREQS)) < ROPE_FREQS, -1.0, 1.0)
    return jnp.cos(ang), jnp.sin(ang) * sign[None, :]


HALF_D = D_MODEL // 2
GU_CHUNK = 512
HI_MASK = 0xFFFF0000
ROW_BITS = 14
assert N_ROWS <= 1 << ROW_BITS


def _pack_halves(y):
    n = y.shape[1] // 2
    lo = lax.bitcast_convert_type(y[:, :n].astype(BF16).astype(F32), jnp.uint32)
    hi = lax.bitcast_convert_type(y[:, n:].astype(BF16).astype(F32), jnp.uint32)
    return (lo >> 16) | (hi & jnp.uint32(HI_MASK))


def _unpack_halves(w):
    lo = lax.bitcast_convert_type(w << 16, F32)
    hi = lax.bitcast_convert_type(w & jnp.uint32(HI_MASK), F32)
    return lo, hi


def _moe_kernel(te_ref, nu_ref, r0_ref, tv_ref, plan_ref, x_hbm, wgu_ref, wdn_ref, y_hbm,
                xbuf, ybuf, pad_s, wgu_s, wdn_s, gsem, ssem):
    t = pl.program_id(0)
    n_used = nu_ref[0]
    slot = t % 2
    active = t < n_used

    def gather_rows(tile, dst_slot):
        base = r0_ref[tile]
        for r in range(MOE_TILE):
            src = plan_ref[base + r] & ((1 << ROW_BITS) - 1)
            pltpu.make_async_copy(x_hbm.at[pl.ds(src, 1)], xbuf.at[dst_slot, pl.ds(r, 1)],
                                  gsem.at[dst_slot]).start()

    def scatter_rows(tile, src_slot):
        base = r0_ref[tile]
        n_valid = tv_ref[tile]

        def body(r, carry):
            dst = plan_ref[base + r] >> ROW_BITS
            pltpu.make_async_copy(ybuf.at[src_slot, pl.ds(r, 1)], y_hbm.at[pl.ds(dst, 1)],
                                  ssem.at[src_slot]).start()
            return carry

        def pad_body(r, carry):
            pltpu.make_async_copy(ybuf.at[src_slot, pl.ds(r, 1)], pad_s.at[src_slot, pl.ds(r, 1)],
                                  ssem.at[src_slot]).start()
            return carry

        @pl.when(n_valid == MOE_TILE)
        def _():
            for r in range(MOE_TILE):
                body(r, 0)

        @pl.when(n_valid < MOE_TILE)
        def _():
            lax.fori_loop(0, n_valid, body, 0)
            lax.fori_loop(n_valid, MOE_TILE, pad_body, 0)

    def wait_gather(s):
        pltpu.make_async_copy(x_hbm.at[pl.ds(0, MOE_TILE)], xbuf.at[s], gsem.at[s]).wait()

    def wait_scatter(s):
        pltpu.make_async_copy(ybuf.at[s], y_hbm.at[pl.ds(0, MOE_TILE)], ssem.at[s]).wait()

    @pl.when(t == 0)
    def _():
        gather_rows(0, 0)

    @pl.when(jnp.logical_and(t >= 2, t - 2 < n_used))
    def _():
        wait_scatter(slot)

    prev = te_ref[jnp.maximum(t - 1, 0)]

    @pl.when(jnp.logical_or(t == 0, te_ref[t] != prev))
    def _():
        wgu_s[...] = wgu_ref[...].astype(BF16)
        wdn_s[...] = wdn_ref[...].astype(BF16)

    @pl.when(active)
    def _():
        wait_gather(slot)
        base_n = r0_ref[jnp.minimum(t + 1, n_used - 1)]
        lo, hi = _unpack_halves(xbuf[slot])
        halves = (lo.astype(BF16), hi.astype(BF16))
        n_chunks = 2 * HALF_D // GU_CHUNK
        rows_per_chunk = MOE_TILE // n_chunks
        gu = None
        for c in range(n_chunks):
            half, off = divmod(c * GU_CHUNK, HALF_D)
            d = jnp.dot(halves[half][:, off:off + GU_CHUNK],
                        wgu_s[half * HALF_D + off:half * HALF_D + off + GU_CHUNK],
                        preferred_element_type=F32)
            gu = d if gu is None else gu + d
            for r in range(c * rows_per_chunk, (c + 1) * rows_per_chunk):
                src = plan_ref[base_n + r] & ((1 << ROW_BITS) - 1)
                pltpu.make_async_copy(x_hbm.at[pl.ds(src, 1)], xbuf.at[1 - slot, pl.ds(r, 1)],
                                      gsem.at[1 - slot]).start()
        gate = gu[:, :EXPERT_W]
        act = (gate * jax.nn.sigmoid(gate) * gu[:, EXPERT_W:]).astype(BF16)
        ybuf[slot] = _pack_halves(jnp.dot(act, wdn_s[...], preferred_element_type=F32))
        scatter_rows(t, slot)

    @pl.when(t == pl.num_programs(0) - 1)
    def _():
        wait_gather(n_used % 2)

        @pl.when(jnp.logical_and(t >= 1, t - 1 < n_used))
        def _():
            wait_scatter(1 - slot)


def moe_grouped(x_packed, tile_expert, n_used, tile_row0, tile_valid, plan, w_gu, w_down, layer):
    n_tok = x_packed.shape[0]
    n_tiles = tile_expert.shape[0]
    n_out = n_tok * TOP_K
    buf = pltpu.VMEM((2, MOE_TILE, HALF_D), jnp.uint32)
    wmap = lambda t, te, nu, r0, tv, plan: (layer, te[t], 0, 0)
    return pl.pallas_call(
        _moe_kernel,
        name="moe_experts",
        grid_spec=pltpu.PrefetchScalarGridSpec(
            num_scalar_prefetch=5,
            grid=(n_tiles,),
            in_specs=[pl.BlockSpec(memory_space=pl.ANY),
                      pl.BlockSpec((None, None, D_MODEL, 2 * EXPERT_W), wmap),
                      pl.BlockSpec((None, None, EXPERT_W, D_MODEL), wmap)],
            out_specs=pl.BlockSpec(memory_space=pl.ANY),
            scratch_shapes=[buf, buf, buf,
                            pltpu.VMEM((D_MODEL, 2 * EXPERT_W), BF16),
                            pltpu.VMEM((EXPERT_W, D_MODEL), BF16),
                            pltpu.SemaphoreType.DMA((2,)),
                            pltpu.SemaphoreType.DMA((2,))]),
        out_shape=jax.ShapeDtypeStruct((n_out, HALF_D), jnp.uint32),
        compiler_params=pltpu.CompilerParams(
            dimension_semantics=("arbitrary",), vmem_limit_bytes=VMEM_LIMIT_BYTES),
    )(tile_expert, n_used, tile_row0, tile_valid, plan, x_packed, w_gu, w_down)


def _combine_kernel(w_ref, *refs):
    y_refs, o_ref = refs[:TOP_K], refs[TOP_K]
    w = w_ref[...]
    acc_lo = acc_hi = None
    for k in range(TOP_K):
        lo, hi = _unpack_halves(y_refs[k][...])
        wk = w[:, k:k + 1]
        acc_lo = lo * wk if acc_lo is None else acc_lo + lo * wk
        acc_hi = hi * wk if acc_hi is None else acc_hi + hi * wk
    o_ref[:, :HALF_D] = acc_lo.astype(o_ref.dtype)
    o_ref[:, HALF_D:] = acc_hi.astype(o_ref.dtype)


def moe_combine(y, w, t, tm=256):
    blocks = t // tm
    return pl.pallas_call(
        _combine_kernel,
        name="moe_combine",
        grid=(blocks,),
        in_specs=[pl.BlockSpec((tm, TOP_K), lambda i: (i, 0))]
                 + [pl.BlockSpec((tm, HALF_D), lambda i, k=k: (k * blocks + i, 0)) for k in range(TOP_K)],
        out_specs=pl.BlockSpec((tm, D_MODEL), lambda i: (i, 0)),
        out_shape=jax.ShapeDtypeStruct((t, D_MODEL), BF16),
        compiler_params=pltpu.CompilerParams(
            dimension_semantics=("parallel",), vmem_limit_bytes=VMEM_LIMIT_BYTES),
    )(w, *([y] * TOP_K))


def route(h, w_router, router_bias):
    logits = jnp.dot(h.astype(F32), w_router, precision=lax.Precision.HIGHEST)
    scores = jax.nn.sigmoid(logits)
    sel = scores + router_bias
    grp = sel.reshape(-1, N_EXPERT_GROUPS, N_EXPERTS // N_EXPERT_GROUPS)
    grp_score = jnp.sum(lax.top_k(grp, 2)[0], axis=-1)
    _, gidx = lax.top_k(grp_score, TOPK_GROUPS)
    gmask = jnp.any(gidx[..., :, None] == jnp.arange(N_EXPERT_GROUPS), axis=-2)
    emask = jnp.repeat(gmask, N_EXPERTS // N_EXPERT_GROUPS, axis=-1)
    _, eidx = lax.top_k(jnp.where(emask, sel, -jnp.inf), TOP_K)
    w = jnp.take_along_axis(scores, eidx, axis=-1)
    w = w / jnp.sum(w, axis=-1, keepdims=True) * ROUTED_SCALE
    return eidx, w


def dispatch_plan(eidx):
    t = eidx.shape[0]
    n_assign = eidx.size
    n_tiles = n_assign // MOE_TILE + N_EXPERTS
    eflat = eidx.reshape(n_assign).astype(jnp.int32)
    _, order = lax.sort_key_val(eflat, jnp.arange(n_assign, dtype=jnp.int32))
    counts = jnp.sum((eflat[:, None] == jnp.arange(N_EXPERTS, dtype=jnp.int32)[None, :]).astype(jnp.int32), axis=0)
    tiles_per = (counts + MOE_TILE - 1) // MOE_TILE
    tile_end = jnp.cumsum(tiles_per)
    tile_start = tile_end - tiles_per
    grp_start = jnp.cumsum(counts) - counts
    n_used = tile_end[-1:]
    tile_ids = jnp.arange(n_tiles, dtype=jnp.int32)
    tile_expert = jnp.sum((tile_ids[:, None] >= tile_end[None, :]).astype(jnp.int32), axis=1)
    tile_expert = jnp.minimum(tile_expert, tile_expert[n_used[0] - 1])
    in_expert = (tile_ids - tile_start[tile_expert]) * MOE_TILE
    used = tile_ids < n_used[0]
    tile_row0 = jnp.where(used, grp_start[tile_expert] + in_expert, 0)
    tile_valid = jnp.where(used, jnp.clip(counts[tile_expert] - in_expert, 0, MOE_TILE), 0)
    token = order // TOP_K
    plan = (((order % TOP_K) * t + token) << ROW_BITS) | token
    filler = jnp.arange(MOE_TILE, dtype=jnp.int32) % t
    return jnp.concatenate([plan, filler]), tile_expert, tile_row0, tile_valid, n_used


def routed_experts(h, h_packed, p, layer):
    t = h.shape[0]
    eidx, w = route(h, p['w_router'], p['router_bias'])
    plan, tile_expert, tile_row0, tile_valid, n_used = dispatch_plan(eidx)
    y = moe_grouped(h_packed, tile_expert, n_used, tile_row0, tile_valid, plan, p['w_exp_gu'], p['w_exp_down'],
                    layer)
    return moe_combine(y, w, t)


def dft_tables(l, scale):
    f = 1
    while f * f < l:
        f *= 2
    assert l % f == 0
    k = jnp.arange(l, dtype=jnp.int32)[None, :]
    j1 = jnp.arange(l // f, dtype=jnp.int32)[:, None]
    j2 = jnp.arange(f, dtype=jnp.int32)[:, None]
    alpha = (2.0 * math.pi / l) * ((j1 * f * k) % l).astype(F32)
    beta = (2.0 * math.pi / l) * ((j2 * k) % l).astype(F32)
    ca, sa, cb, sb = jnp.cos(alpha)[:, None], jnp.sin(alpha)[:, None], jnp.cos(beta)[None], jnp.sin(beta)[None]
    cos = (ca * cb - sa * sb).reshape(l, l) * scale
    sin = (sa * cb + ca * sb).reshape(l, l) * scale
    return cos, sin


def channel_dft_weight():
    cos, sin = dft_tables(FNET_GW, 1.0)
    eye = jnp.eye(FNET_GROUPS, dtype=F32)
    return jnp.concatenate([jnp.kron(eye, cos), jnp.kron(eye, sin)], axis=1)


def _dft_seq_kernel(c_ref, s_ref, p_ref, q_ref, o_ref):
    o_ref[...] = (jnp.dot(c_ref[...], p_ref[...], preferred_element_type=F32)
                  - jnp.dot(s_ref[...], q_ref[...], preferred_element_type=F32)).astype(o_ref.dtype)


def fourier_mix(pq, tables, *, row_start, l, n_seq, tm=ROW_TILE, tn=COL_TILE):
    cos, sin = tables
    tm = min(tm, l)
    seq0 = row_start // l
    nj = FNET_W // tn
    return pl.pallas_call(
        _dft_seq_kernel,
        name="fourier_seq",
        grid=(n_seq, nj, l // tm),
        in_specs=[pl.BlockSpec((tm, l), lambda b, j, i: (i, 0)),
                  pl.BlockSpec((tm, l), lambda b, j, i: (i, 0)),
                  pl.BlockSpec((l, tn), lambda b, j, i: (seq0 + b, j)),
                  pl.BlockSpec((l, tn), lambda b, j, i: (seq0 + b, nj + j))],
        out_specs=pl.BlockSpec((tm, tn), lambda b, j, i: (b * (l // tm) + i, j)),
        out_shape=jax.ShapeDtypeStruct((n_seq * l, FNET_W), BF16),
        compiler_params=pltpu.CompilerParams(
            dimension_semantics=("parallel", "parallel", "arbitrary"), vmem_limit_bytes=VMEM_LIMIT_BYTES),
    )(cos, sin, pq, pq)


CONF_HALO = 16


CONF_ROWS = 64


def _conformer_kernel(prev_ref, a_ref, g_ref, next_ref, pg_ref, ng_ref, w_ref, b_ref, lg_ref, lb_ref,
                      o_ref, v_s, c_s, *, tl):
    i = pl.program_id(1)
    last = pl.num_programs(1) - 1

    def glu(a, g):
        a = a.astype(F32)
        return a * jax.nn.sigmoid(g.astype(F32))

    v_s[0:CONF_HALO] = jnp.where(i > 0, glu(prev_ref[...], pg_ref[...]), 0.0)
    v_s[CONF_HALO:CONF_HALO + tl] = glu(a_ref[...], g_ref[...])
    v_s[CONF_HALO + tl:] = jnp.where(i < last, glu(next_ref[...], ng_ref[...]), 0.0)
    for c in range(CONF_W // LANE):
        lanes = slice(c * LANE, (c + 1) * LANE)
        w = w_ref[:, lanes]
        for rb in range(tl // CONF_ROWS):
            acc = jnp.zeros((CONF_ROWS, LANE), F32) + b_ref[:, lanes]
            for k in range(CONF_K):
                off = rb * CONF_ROWS + CONF_HALO - CONF_K // 2 + k
                acc = acc + v_s[off:off + CONF_ROWS, lanes] * w[k:k + 1]
            c_s[rb * CONF_ROWS:(rb + 1) * CONF_ROWS, lanes] = acc
    acc = c_s[...]
    mu = jnp.mean(acc, axis=-1, keepdims=True)
    cen = acc - mu
    var = jnp.mean(cen * cen, axis=-1, keepdims=True)
    y = cen * lax.rsqrt(var + EPS) * lg_ref[...] + lb_ref[...]
    o_ref[...] = (y * jax.nn.sigmoid(y)).astype(o_ref.dtype)


def conformer_conv(proj, p, *, col_block, row_start, l, n_seq, tl=256):
    tl = min(tl, l)
    nt = l // tl
    hb = tl // CONF_HALO
    first = row_start // CONF_HALO

    def main(c):
        return pl.BlockSpec((tl, CONF_W), lambda b, i: (row_start // tl + b * nt + i, col_block + c))

    def prev(c):
        return pl.BlockSpec((CONF_HALO, CONF_W),
                            lambda b, i: (first + jnp.maximum((b * nt + i) * hb - 1, 0), col_block + c))

    def nxt(c):
        return pl.BlockSpec((CONF_HALO, CONF_W),
                            lambda b, i: (first + jnp.minimum((b * nt + i + 1) * hb, n_seq * nt * hb - 1),
                                          col_block + c))

    vec = pl.BlockSpec((1, CONF_W), lambda b, i: (0, 0))
    kern = functools.partial(_conformer_kernel, tl=tl)
    return pl.pallas_call(
        kern,
        name="conformer_conv",
        grid=(n_seq, nt),
        in_specs=[prev(0), main(0), main(1), nxt(0), prev(1), nxt(1),
                  pl.BlockSpec((CONF_K, CONF_W), lambda b, i: (0, 0)), vec, vec, vec],
        out_specs=pl.BlockSpec((tl, CONF_W), lambda b, i: (b * nt + i, 0)),
        out_shape=jax.ShapeDtypeStruct((n_seq * l, CONF_W), BF16),
        scratch_shapes=[pltpu.VMEM((tl + 2 * CONF_HALO, CONF_W), F32), pltpu.VMEM((tl, CONF_W), F32)],
        compiler_params=pltpu.CompilerParams(
            dimension_semantics=("parallel", "parallel"), vmem_limit_bytes=VMEM_LIMIT_BYTES),
    )(proj, proj, proj, proj, proj, proj, p['conf_conv_w'], p['conf_conv_b'].reshape(1, CONF_W),
      p['conf_ln_g'].reshape(1, CONF_W), p['conf_ln_b'].reshape(1, CONF_W))


SSD_XBC = SSD_W + 2 * SSD_BC
SSD_HALO = 16
SSD_CONV_TILE = 256
HEADS_PER_GROUP = SSD_HEADS // SSD_GROUPS
GROUP_W = HEADS_PER_GROUP * SSD_HD


def _ssd_conv_kernel(prev_ref, x_ref, next_ref, w_ref, b_ref, o_ref, v_s):
    tl = SSD_CONV_TILE
    i = pl.program_id(0)
    lat_tiles = N_LAT // tl
    per_seq = jnp.where(i < lat_tiles, SEQ // tl, CTX_LEN // tl)
    pos = jnp.where(i < lat_tiles, i % (SEQ // tl), (i - lat_tiles) % (CTX_LEN // tl))
    v_s[0:SSD_HALO] = jnp.where(pos > 0, prev_ref[...].astype(F32), 0.0)
    v_s[SSD_HALO:SSD_HALO + tl] = x_ref[...].astype(F32)
    v_s[SSD_HALO + tl:] = jnp.where(pos < per_seq - 1, next_ref[...].astype(F32), 0.0)
    for c in range(SSD_XBC // LANE):
        lanes = slice(c * LANE, (c + 1) * LANE)
        w = w_ref[:, lanes]
        for rb in range(tl // CONF_ROWS):
            acc = jnp.zeros((CONF_ROWS, LANE), F32) + b_ref[:, lanes]
            for k in range(SSD_CONV):
                off = rb * CONF_ROWS + SSD_HALO - SSD_CONV // 2 + k
                acc = acc + v_s[off:off + CONF_ROWS, lanes] * w[k:k + 1]
            o_ref[rb * CONF_ROWS:(rb + 1) * CONF_ROWS, lanes] = (acc * jax.nn.sigmoid(acc)).astype(o_ref.dtype)


def ssd_conv(proj, conv_w, conv_b, *, col_block):
    tl = SSD_CONV_TILE
    hb = tl // SSD_HALO
    n_tiles = N_ROWS // tl
    return pl.pallas_call(
        _ssd_conv_kernel,
        name="ssd_conv",
        grid=(n_tiles,),
        in_specs=[pl.BlockSpec((SSD_HALO, SSD_XBC), lambda i: (jnp.maximum(i * hb - 1, 0), col_block)),
                  pl.BlockSpec((tl, SSD_XBC), lambda i: (i, col_block)),
                  pl.BlockSpec((SSD_HALO, SSD_XBC),
                               lambda i: (jnp.minimum((i + 1) * hb, n_tiles * hb - 1), col_block)),
                  pl.BlockSpec((SSD_CONV, SSD_XBC), lambda i: (0, 0)),
                  pl.BlockSpec((1, SSD_XBC), lambda i: (0, 0))],
        out_specs=pl.BlockSpec((tl, SSD_XBC), lambda i: (i, 0)),
        out_shape=jax.ShapeDtypeStruct((N_ROWS, SSD_XBC), BF16),
        scratch_shapes=[pltpu.VMEM((tl + 2 * SSD_HALO, SSD_XBC), F32)],
        compiler_params=pltpu.CompilerParams(
            dimension_semantics=("parallel",), vmem_limit_bytes=VMEM_LIMIT_BYTES),
    )(proj, proj, proj, conv_w, conv_b.reshape(1, SSD_XBC))


def _split3(v):
    hi = v.astype(BF16)
    r1 = v - hi.astype(F32)
    mid = r1.astype(BF16)
    lo = (r1 - mid.astype(F32)).astype(BF16)
    return hi, mid, lo


def _ssd_scan_kernel(x_ref, b_ref, c_ref, dt_ref, a_ref, bias_ref, y_ref, st, *, reverse):
    @pl.when(pl.program_id(1) == 0)
    def _():
        st[...] = jnp.zeros_like(st)

    t = SSD_CHUNK
    col0 = SSD_HEADS if reverse else 0
    dt_in = dt_ref[:, col0:col0 + SSD_HEADS] + bias_ref[...]
    dt = jnp.maximum(dt_in, 0.0) + jnp.log(1.0 + jnp.exp(-jnp.abs(dt_in)))
    ad = dt * a_ref[...]
    row = lax.broadcasted_iota(jnp.int32, (t, t), 0)
    col = lax.broadcasted_iota(jnp.int32, (t, t), 1)
    keep = (col >= row) if reverse else (col <= row)
    tri = jnp.where(keep, 1.0, 0.0).astype(BF16)
    cs = sum(jnp.dot(tri, piece, preferred_element_type=F32) for piece in _split3(ad))
    cs_t = cs.T
    total = cs[0:1] if reverse else cs[t - 1:t]
    w_end = jnp.exp(total - cs)
    e_cs = jnp.exp(cs)
    e_tot = jnp.exp(total)
    head_of_lane = lax.broadcasted_iota(jnp.int32, (1, GROUP_W), 1) // SSD_HD

    def per_lane(v, g):
        out = v[:, g * HEADS_PER_GROUP:g * HEADS_PER_GROUP + 1]
        for hl in range(1, HEADS_PER_GROUP):
            h = g * HEADS_PER_GROUP + hl
            out = jnp.where(head_of_lane >= hl, v[:, h:h + 1], out)
        return out

    for g in range(SSD_GROUPS):
        bg = b_ref[:, g * SSD_STATE:(g + 1) * SSD_STATE]
        cg = c_ref[:, g * SSD_STATE:(g + 1) * SSD_STATE]
        cols = slice(g * GROUP_W, (g + 1) * GROUP_W)
        gmat = lax.dot_general(cg, bg, (((1,), (1,)), ((), ())), preferred_element_type=F32)
        st_g = st[g]
        xd = x_ref[:, cols].astype(F32) * per_lane(dt, g)
        xd_b = xd.astype(BF16)
        y_g = jnp.dot(cg, st_g.astype(BF16), preferred_element_type=F32) * per_lane(e_cs, g)
        y_diag = None
        for hl in range(HEADS_PER_GROUP):
            h = g * HEADS_PER_GROUP + hl
            lmat = jnp.exp(jnp.where(keep, cs[:, h:h + 1] - cs_t[h:h + 1, :], -jnp.inf))
            r = jnp.dot((gmat * lmat).astype(BF16), xd_b, preferred_element_type=F32)
            y_diag = r if y_diag is None else jnp.where(head_of_lane >= hl, r, y_diag)
        y_ref[:, cols] = (y_g + y_diag).astype(y_ref.dtype)
        upd = lax.dot_general(bg, (xd * per_lane(w_end, g)).astype(BF16), (((0,), (0,)), ((), ())),
                              preferred_element_type=F32)
        st[g] = st_g * per_lane(e_tot, g) + upd


def ssd_scan(u, dt_raw, a_log, dt_bias, *, reverse):
    d = 1 if reverse else 0
    ctx_chunks = CTX_LEN // SSD_CHUNK
    lat_chunks = SEQ // SSD_CHUNK
    n_chunks = ctx_chunks + lat_chunks

    def rows(b, s):
        ctx_i = (ctx_chunks - 1 - s) if reverse else s
        lat_i = (n_chunks - 1 - s) if reverse else (s - ctx_chunks)
        return jnp.where(s < ctx_chunks, N_LAT // SSD_CHUNK + b * ctx_chunks + ctx_i, b * lat_chunks + lat_i)

    xw = SSD_W // SSD_BC
    kern = functools.partial(_ssd_scan_kernel, reverse=reverse)
    return pl.pallas_call(
        kern,
        name="ssd_scan",
        grid=(BATCH, n_chunks),
        in_specs=[pl.BlockSpec((SSD_CHUNK, SSD_W), lambda b, s: (rows(b, s), 0)),
                  pl.BlockSpec((SSD_CHUNK, SSD_BC), lambda b, s: (rows(b, s), xw)),
                  pl.BlockSpec((SSD_CHUNK, SSD_BC), lambda b, s: (rows(b, s), xw + 1)),
                  pl.BlockSpec((SSD_CHUNK, LANE), lambda b, s: (rows(b, s), 0)),
                  pl.BlockSpec((1, SSD_HEADS), lambda b, s: (0, 0)),
                  pl.BlockSpec((1, SSD_HEADS), lambda b, s: (0, 0))],
        out_specs=pl.BlockSpec((SSD_CHUNK, SSD_W), lambda b, s: (rows(b, s), 0)),
        out_shape=jax.ShapeDtypeStruct((N_ROWS, SSD_W), BF16),
        scratch_shapes=[pltpu.VMEM((SSD_GROUPS, SSD_STATE, GROUP_W), F32)],
        compiler_params=pltpu.CompilerParams(
            dimension_semantics=("parallel", "arbitrary"), vmem_limit_bytes=VMEM_LIMIT_BYTES),
    )(u, u, u, dt_raw, -jnp.exp(a_log[d]).reshape(1, SSD_HEADS), dt_bias[d].reshape(1, SSD_HEADS))


def _ssd_finish_kernel(yf_ref, yb_ref, x_ref, z_ref, d_ref, g_ref, o_ref):
    z = z_ref[...].astype(F32)
    y = yf_ref[...].astype(F32) + yb_ref[...].astype(F32) + d_ref[...] * x_ref[...].astype(F32)
    y = y * (z * jax.nn.sigmoid(z))
    gw = SSD_W // SSD_GROUPS
    for g in range(SSD_GROUPS):
        cols = slice(g * gw, (g + 1) * gw)
        yg = y[:, cols]
        yg = yg * lax.rsqrt(jnp.mean(yg * yg, axis=-1, keepdims=True) + EPS)
        o_ref[:, cols] = (yg * g_ref[:, cols]).astype(o_ref.dtype)


def ssd_finish(y_f, y_b, u, proj_mix, ssd_d, norm_g, *, m, z_col_block, tm=ROW_TILE):
    row = pl.BlockSpec((tm, SSD_W), lambda i: (i, 0))
    vec = pl.BlockSpec((1, SSD_W), lambda i: (0, 0))
    return pl.pallas_call(
        _ssd_finish_kernel,
        name="ssd_finish",
        grid=(m // tm,),
        in_specs=[row, row, row, pl.BlockSpec((tm, SSD_W), lambda i: (i, z_col_block)), vec, vec],
        out_specs=row,
        out_shape=jax.ShapeDtypeStruct((m, SSD_W), BF16),
        compiler_params=pltpu.CompilerParams(
            dimension_semantics=("parallel",), vmem_limit_bytes=VMEM_LIMIT_BYTES),
    )(y_f, y_b, u, proj_mix, jnp.repeat(ssd_d, SSD_HD).reshape(1, SSD_W), norm_g.reshape(1, SSD_W))


def _lat(a):
    return a[:N_LAT].reshape(BATCH, SEQ, a.shape[-1])


def _ctx(a):
    return a[N_LAT:N_ROWS].reshape(BATCH, CTX_LEN, a.shape[-1])


def _rows(lat, ctx=None):
    lat = lat.reshape(N_LAT, lat.shape[-1])
    if ctx is None:
        return lat
    return jnp.concatenate([lat, ctx.reshape(N_CTX, ctx.shape[-1])], axis=0)


def _merge_epilogue(dots, extras):
    acc = None
    for d, g in zip(dots, extras):
        term = jax.nn.sigmoid(g.astype(F32)) * d
        acc = term if acc is None else acc + term
    return acc


def _residual_epilogue(dots, extras):
    x, g = extras
    return x + g * dots[0]


def _ffn_out_epilogue(dots, extras):
    x, routed, g = extras
    return x + g * (dots[0] + routed)


def _glu_epilogue(dots, extras):
    gate, up = dots
    return gate * jax.nn.sigmoid(gate) * up


def _norm_kernel(x_ref, g_ref, sc_ref, sh_ref, o_ref, *packed_ref):
    x = x_ref[...]
    y = x * lax.rsqrt(jnp.mean(x * x, axis=-1, keepdims=True) + EPS) * g_ref[...]
    y = y * (1.0 + sc_ref[...]) + sh_ref[...]
    o_ref[...] = y.astype(o_ref.dtype)
    if packed_ref:
        packed_ref[0][...] = _pack_halves(y)


def rms_modulate(x, g, scale, shift, *, m, out_dtype=BF16, packed=False, tm=256):
    d = x.shape[1]
    seg_spec = pl.BlockSpec((None, 1, d), lambda i: (_segment_of_row_tile(i, tm), 0, 0))
    out_specs = [pl.BlockSpec((tm, d), lambda i: (i, 0))]
    out_shape = [jax.ShapeDtypeStruct((m, d), out_dtype)]
    if packed:
        out_specs.append(pl.BlockSpec((tm, d // 2), lambda i: (i, 0)))
        out_shape.append(jax.ShapeDtypeStruct((m, d // 2), jnp.uint32))
    outs = pl.pallas_call(
        _norm_kernel,
        name="rms_modulate",
        grid=(m // tm,),
        in_specs=[pl.BlockSpec((tm, d), lambda i: (i, 0)),
                  pl.BlockSpec((1, d), lambda i: (0, 0)),
                  seg_spec, seg_spec],
        out_specs=out_specs,
        out_shape=out_shape,
        compiler_params=pltpu.CompilerParams(
            dimension_semantics=("parallel",), vmem_limit_bytes=VMEM_LIMIT_BYTES),
    )(x, g.reshape(1, d), scale.reshape(-1, 1, d), shift.reshape(-1, 1, d))
    return outs if packed else outs[0]


def token_mixers(h, p, layer_idx, ctx_out, m_full):
    w_all = p['w_all']
    lp = (layer_idx,)
    proj_a = project(h, w_all, W_STATE, A_W, m=N_ROWS, prefix=lp, tn=PROJ_COL_TILE, name="proj_state")
    proj_dt = project(h, w_all, W_DT, LANE, m=N_ROWS, prefix=lp, out_dtype=F32, tn=LANE, name="proj_dt")
    proj_b = project(h, w_all, W_MIX, M_W, m=m_full, prefix=lp, tn=PROJ_COL_TILE, name="proj_mix")
    proj_g = project(h, w_all, W_GATE, N_BRANCH * D_MODEL, m=m_full, prefix=lp, tn=PROJ_COL_TILE,
                     name="proj_gate")

    lam_init = 0.8 - 0.6 * math.exp(-0.3 * layer_idx)
    lam = (jnp.exp(jnp.sum(p['lam_q1'] * p['lam_k1'])) - jnp.exp(jnp.sum(p['lam_q2'] * p['lam_k2'])) + lam_init)
    cos_l, sin_l = rope_tables(SEQ)
    cos_c, sin_c = jnp.ones((CTX_LEN, LANE), F32), jnp.zeros((CTX_LEN, LANE), F32)
    kv_lat = _lat(proj_a[:, :A_XBC])
    kv_ctx = _ctx(proj_a[:, :A_XBC])
    kv_all = jnp.concatenate([kv_lat, kv_ctx], axis=1)
    rope_k = (jnp.concatenate([cos_l, cos_c]), jnp.concatenate([sin_l, sin_c]))
    y_diff = diff_attention(_lat(proj_b), kv_all, (cos_l, sin_l), rope_k, lam, p['g_subln'], lam_init,
                            q_col_block=M_Q // LANE)
    if ctx_out:
        yc_diff = diff_attention(_ctx(proj_b), kv_ctx, (cos_c, sin_c), (cos_c, sin_c), lam, p['g_subln'],
                                 lam_init, q_col_block=M_Q // LANE)

    u = ssd_conv(proj_a, p['ssd_conv_w'], p['ssd_conv_b'], col_block=A_XBC // SSD_XBC)
    y_f = ssd_scan(u, proj_dt, p['ssd_a_log'], p['ssd_dt_bias'], reverse=False)
    y_b = ssd_scan(u, proj_dt, p['ssd_a_log'], p['ssd_dt_bias'], reverse=True)
    y_ssd = ssd_finish(y_f, y_b, u, proj_b, p['ssd_d'], p['ssd_norm_g'], m=m_full, z_col_block=M_SZ // SSD_W)

    pq = ws_matmul([(proj_b, FNET_W, M_FN // FNET_W)], [(p['w_chan_dft'], (), lambda j: j)], [0], _ident,
                   m=m_full, n=2 * FNET_W, out_dtype=BF16, name="fourier_chan")
    y_fn = fourier_mix(pq, p['dft_lat'], row_start=0, l=SEQ, n_seq=BATCH)
    y_cf = conformer_conv(proj_b, p, col_block=M_CF // CONF_W, row_start=0, l=SEQ, n_seq=BATCH)

    if ctx_out:
        yc_fn = fourier_mix(pq, p['dft_ctx'], row_start=N_LAT, l=CTX_LEN, n_seq=BATCH)
        yc_cf = conformer_conv(proj_b, p, col_block=M_CF // CONF_W, row_start=N_LAT, l=CTX_LEN, n_seq=BATCH)
        branches = [jnp.concatenate([y_fn, yc_fn]), _rows(y_diff, yc_diff), y_ssd, jnp.concatenate([y_cf, yc_cf])]
    else:
        branches = [y_fn, _rows(y_diff), y_ssd, y_cf]
    return branches, proj_g


def trunk_layer(x, mods, p, layer_idx, ctx_out):
    sh1, sc1, g1, sh2, sc2, g2 = jnp.split(mods, 6, axis=-1)
    m_full = N_ROWS if ctx_out else N_LAT
    lp = (layer_idx,)
    h = rms_modulate(x, p['g_mix'], sc1, sh1, m=N_ROWS)
    branches, proj_g = token_mixers(h, p, layer_idx, ctx_out, m_full)

    gate_blocks = D_MODEL // COL_TILE
    acc = ws_matmul(
        branches, [(p['w_branch'], (layer_idx, i), lambda j: j) for i in range(N_BRANCH)], list(range(N_BRANCH)),
        _merge_epilogue,
        extras=[('tile', proj_g, lambda j, i=i: i * gate_blocks + j) for i in range(N_BRANCH)],
        m=m_full, n=D_MODEL, out_dtype=BF16, name="merge_branches")
    x = ws_matmul([acc], [(p['w_out'], lp, lambda j: j)], [0], _residual_epilogue,
                  extras=[('tile', x, lambda j: j), ('seg', g1.reshape(3, 1, D_MODEL))],
                  m=m_full, n=D_MODEL, out_dtype=F32, name="mix_out")

    h2, h2_packed = rms_modulate(x, p['g_ffn'], sc2, sh2, m=m_full, packed=True)
    routed = routed_experts(h2, h2_packed, p, layer_idx)
    up_blocks = SHARED_W // COL_TILE
    act = ws_matmul([h2], [(p['w_sh_gu'], lp, lambda j: j), (p['w_sh_gu'], lp, lambda j: j + up_blocks)], [0, 0],
                    _glu_epilogue, m=m_full, n=SHARED_W, out_dtype=BF16, name="shared_glu")
    x = ws_matmul([act], [(p['w_sh_down'], lp, lambda j: j)], [0], _ffn_out_epilogue,
                  extras=[('tile', x, lambda j: j), ('tile', routed, lambda j: j),
                          ('seg', g2.reshape(3, 1, D_MODEL))],
                  m=m_full, n=D_MODEL, out_dtype=F32, name="ffn_out")
    return x


def kernel(x, c, ctx, c_ctx, w_ada, b_ada, g_mix, w_in, lam_q1, lam_k1, lam_q2, lam_k2, g_subln, ssd_conv_w, ssd_conv_b, ssd_dt_bias, ssd_a_log, ssd_d, ssd_norm_g, conf_conv_w, conf_conv_b, conf_ln_g, conf_ln_b, w_branch, w_out, g_ffn, w_router, router_bias, w_exp_gu, w_exp_down, w_sh_gu, w_sh_down, g_final):
    cc = jnp.concatenate([jax.nn.silu(c), jax.nn.silu(c_ctx)[None]], axis=0)
    cc = jnp.pad(cc, ((0, 16 - cc.shape[0]), (0, 0))).astype(BF16)
    mods = ada_modulation(cc, w_ada, b_ada)[:, :BATCH + 1]

    dft_lat = tuple(t.astype(BF16) for t in dft_tables(SEQ, (SEQ * FNET_GW) ** -0.5))
    dft_ctx = tuple(t.astype(BF16) for t in dft_tables(CTX_LEN, (CTX_LEN * FNET_GW) ** -0.5))
    w_chan_dft = channel_dft_weight()
    w_all = jnp.concatenate(
        [w_in[:, :, K_OFF:SDT_OFF], w_in[:, :, SC_OFF:Q_OFF],
         w_in[:, :, CF_OFF:GATE_OFF], w_in[:, :, Q_OFF:CF_OFF],
         w_in[:, :, GATE_OFF:],
         jnp.pad(w_in[:, :, SDT_OFF:SC_OFF], ((0, 0), (0, 0), (0, W_ALL - W_DT - 2 * SSD_HEADS)))],
        axis=2).astype(BF16)

    xa = jnp.concatenate([x.reshape(N_LAT, D_MODEL), ctx.reshape(N_CTX, D_MODEL)], axis=0)
    for l in range(DEPTH):
        p = {
            'g_mix': g_mix[l], 'w_all': w_all,
            'lam_q1': lam_q1[l], 'lam_k1': lam_k1[l], 'lam_q2': lam_q2[l], 'lam_k2': lam_k2[l],
            'g_subln': g_subln[l], 'ssd_conv_w': ssd_conv_w[l], 'ssd_conv_b': ssd_conv_b[l],
            'ssd_dt_bias': ssd_dt_bias[l], 'ssd_a_log': ssd_a_log[l], 'ssd_d': ssd_d[l],
            'ssd_norm_g': ssd_norm_g[l], 'conf_conv_w': conf_conv_w[l], 'conf_conv_b': conf_conv_b[l],
            'conf_ln_g': conf_ln_g[l], 'conf_ln_b': conf_ln_b[l], 'w_branch': w_branch,
            'w_out': w_out, 'g_ffn': g_ffn[l], 'w_router': w_router[l], 'router_bias': router_bias[l],
            'w_exp_gu': w_exp_gu, 'w_exp_down': w_exp_down, 'w_sh_gu': w_sh_gu,
            'w_sh_down': w_sh_down,
            'dft_lat': dft_lat, 'dft_ctx': dft_ctx, 'w_chan_dft': w_chan_dft,
        }
        xa = trunk_layer(xa, mods[l], p, l, l < DEPTH - 1)
    zeros = jnp.zeros((BATCH + 1, D_MODEL), F32)
    y = rms_modulate(xa, g_final, zeros, zeros, m=N_LAT, out_dtype=F32)
    return y.reshape(BATCH, SEQ, D_MODEL)
```
